```python
import math
import jax, jax.numpy as jnp
from jax import lax
import numpy as np

D_MODEL = 1024
BATCH = 8
SEQ = 8192
DEPTH = 4

N_MIXERS = 3
NORM_EPS = 1e-6

HG_HEADS = 8
HG_DK = D_MODEL // HG_HEADS
HG_DV = D_MODEL // HG_HEADS
HG_HK = HG_HEADS * HG_DK
HG_HV = HG_HEADS * HG_DV
HG_CHUNK = 16

FOX_HEADS = 16
FOX_DH = D_MODEL // FOX_HEADS
FOX_BLOCK = 128

SSM_DI = 2 * D_MODEL
SSM_HEADDIM = 64
SSM_HEADS = SSM_DI // SSM_HEADDIM
SSM_GROUPS = 8
SSM_STATE = 128
SSM_CONV = 4
SSM_CHUNK = 64
SSM_CONV_DIM = SSM_DI + 2 * SSM_GROUPS * SSM_STATE
SSM_IN = 2 * SSM_DI + 2 * SSM_GROUPS * SSM_STATE + SSM_HEADS

PEER_HEADS = 8
PEER_NKEYS = 128
PEER_EXPERTS = PEER_NKEYS * PEER_NKEYS
PEER_DKEY = 256
PEER_HALF = PEER_DKEY // 2
PEER_TOPK = 16
PEER_TOKEN_BLOCK = 128

N_HGRN = (DEPTH + 2) // 3
N_FOX = (DEPTH + 1) // 3
N_SSM = DEPTH // 3

kernel_name = 'hgrn2_fox_ssd_peer_interleaved_trunk'


def rmsnorm(x, g, eps=NORM_EPS):
    xf = x.astype(jnp.float32)
    y = xf * lax.rsqrt(jnp.mean(xf * xf, axis=-1, keepdims=True) + eps)
    return (y * g.astype(jnp.float32)).astype(x.dtype)


def hgrn_lower_bounds(logits):
    cum = jnp.cumsum(jax.nn.softmax(logits.astype(jnp.float32), axis=0), axis=0)
    return cum - cum[0:1]


def chunk_gla(q, k, v, log_f, chunk):
    B, T, H, K = q.shape
    V = v.shape[-1]
    n = T // chunk

    def to_chunks(a):
        return a.reshape(B, n, chunk, H, a.shape[-1]).transpose(1, 0, 3, 2, 4)

    causal = jnp.tril(jnp.ones((chunk, chunk), dtype=bool))

    def step(S, inp):
        qc, kc, vc, gc = inp
        b = jnp.cumsum(gc.astype(jnp.float32), axis=2)
        b_last = b[:, :, -1:, :]
        q_dec = qc * jnp.exp(b)
        k_inv = kc * jnp.exp(-b)
        k_dec = kc * jnp.exp(b_last - b)
        A = jnp.where(causal, jnp.einsum('bhtk,bhsk->bhts', q_dec, k_inv), 0.0)
        o = jnp.einsum('bhts,bhsv->bhtv', A, vc) + jnp.einsum('bhtk,bhkv->bhtv', q_dec, S)
        S_new = jnp.exp(b_last[:, :, 0, :])[..., None] * S + jnp.einsum('bhsk,bhsv->bhkv', k_dec, vc)
        return S_new.astype(jnp.float32), o.astype(jnp.float32)

    S0 = jnp.zeros((B, H, K, V), jnp.float32)
    _, o = lax.scan(step, S0, (to_chunks(q), to_chunks(k), to_chunks(v), to_chunks(log_f)))
    return o.transpose(1, 0, 3, 2, 4).reshape(B, T, H, V)


def hgrn2_mixer(h, w_in, g_norm, w_out, lb):
    B, T, _ = h.shape
    q, f_logit, i_in, gate = jnp.split(h @ w_in, [HG_HK, 2 * HG_HK, 2 * HG_HK + HG_HV], axis=-1)
    lbf = lb.astype(jnp.float32)
    f = lbf + (1.0 - lbf) * jax.nn.sigmoid(f_logit.astype(jnp.float32))

    def heads(a, d):
        return a.reshape(B, T, HG_HEADS, d)

    o = chunk_gla(heads(q * (HG_DK ** -0.5), HG_DK), heads(1.0 - f, HG_DK),
                  heads(i_in, HG_DV), heads(jnp.log(f), HG_DK), HG_CHUNK)
    o = rmsnorm(o, g_norm) * jax.nn.silu(heads(gate, HG_DV))
    return o.reshape(B, T, HG_HV).astype(h.dtype) @ w_out


def fox_mixer(h, w_in, b_f, w_out):
    B, T, _ = h.shape
    q, k, v, f_logit = jnp.split(h @ w_in, [D_MODEL, 2 * D_MODEL, 3 * D_MODEL], axis=-1)
    log_f = jax.nn.log_sigmoid((f_logit + b_f).astype(jnp.float32))
    c = jnp.cumsum(log_f, axis=1).transpose(0, 2, 1)

    def heads(a):
        return a.reshape(B, T, FOX_HEADS, FOX_DH).transpose(0, 2, 1, 3)

    q, k, v = heads(q), heads(k), heads(v)
    scale = FOX_DH ** -0.5
    qpos = jnp.arange(FOX_BLOCK)
    outs = []
    for blk in range(T // FOX_BLOCK):
        q0 = blk * FOX_BLOCK
        kv_len = q0 + FOX_BLOCK
        logits = (jnp.einsum('bhtd,bhsd->bhts', q[:, :, q0:kv_len], k[:, :, :kv_len]).astype(jnp.float32) * scale
                  + c[:, :, q0:kv_len, None] - c[:, :, None, :kv_len])
        mask = (q0 + qpos)[:, None] >= jnp.arange(kv_len)[None, :]
        p = jax.nn.softmax(jnp.where(mask, logits, -jnp.inf), axis=-1)
        outs.append(jnp.einsum('bhts,bhsd->bhtd', p.astype(v.dtype), v[:, :, :kv_len]))
    o = jnp.concatenate(outs, axis=2).transpose(0, 2, 1, 3).reshape(B, T, D_MODEL)
    return o.astype(h.dtype) @ w_out


def causal_depthwise_conv(x, w, b):
    W = w.shape[0]
    out = lax.conv_general_dilated(x, w[:, None, :].astype(x.dtype), window_strides=(1,),
                                   padding=[(W - 1, 0)], dimension_numbers=('NWC', 'WIO', 'NWC'),
                                   feature_group_count=x.shape[-1])
    return out + b


def ssd_chunk_scan(x, dA, Bm, Cm, chunk):
    Bsz, T, H, P = x.shape
    G, N = Bm.shape[-2:]
    R = H // G
    n = T // chunk
    xc = x.reshape(Bsz, n, chunk, G, R, P).transpose(1, 0, 2, 3, 4, 5)
    ac = dA.reshape(Bsz, n, chunk, G, R).transpose(1, 0, 3, 4, 2)
    bc = Bm.reshape(Bsz, n, chunk, G, N).transpose(1, 0, 2, 3, 4)
    cc = Cm.reshape(Bsz, n, chunk, G, N).transpose(1, 0, 2, 3, 4)
    causal = jnp.tril(jnp.ones((chunk, chunk), dtype=bool))

    def step(S, inp):
        x_, a_, b_, c_ = inp
        cum = jnp.cumsum(a_.astype(jnp.float32), axis=-1)
        L = jnp.exp(jnp.where(causal, cum[..., :, None] - cum[..., None, :], -jnp.inf))
        cb = jnp.einsum('btgn,bsgn->bgts', c_, b_)
        y_intra = jnp.einsum('bgrts,bsgrp->btgrp', cb[:, :, None] * L, x_)
        y_inter = jnp.einsum('btgn,bgrpn,bgrt->btgrp', c_, S, jnp.exp(cum))
        decay_to_end = jnp.exp(cum[..., -1:] - cum)
        S_new = (jnp.exp(cum[..., -1])[..., None, None] * S
                 + jnp.einsum('bgrs,bsgn,bsgrp->bgrpn', decay_to_end, b_, x_))
        return S_new.astype(jnp.float32), (y_intra + y_inter).astype(jnp.float32)

    S0 = jnp.zeros((Bsz, G, R, P, N), jnp.float32)
    _, y = lax.scan(step, S0, (xc, ac, bc, cc))
    return y.transpose(1, 0, 2, 3, 4, 5).reshape(Bsz, T, H, P)


def ssd_mixer(h, w_in, conv_w, conv_b, dt_bias, a_log, d_skip, g_norm, w_out):
    B, T, _ = h.shape
    z, xbc, dt_raw = jnp.split(h @ w_in, [SSM_DI, SSM_DI + SSM_CONV_DIM], axis=-1)
    xbc = jax.nn.silu(causal_depthwise_conv(xbc, conv_w, conv_b))
    xs, Bm, Cm = jnp.split(xbc, [SSM_DI, SSM_DI + SSM_GROUPS * SSM_STATE], axis=-1)
    xs = xs.reshape(B, T, SSM_HEADS, SSM_HEADDIM)
    Bm = Bm.reshape(B, T, SSM_GROUPS, SSM_STATE)
    Cm = Cm.reshape(B, T, SSM_GROUPS, SSM_STATE)
    dt = jax.nn.softplus((dt_raw + dt_bias).astype(jnp.float32))
    dA = dt * (-jnp.exp(a_log.astype(jnp.float32)))
    y = ssd_chunk_scan(xs * dt[..., None], dA, Bm, Cm, SSM_CHUNK)
    y = (y + xs * d_skip[:, None]).reshape(B, T, SSM_DI)
    gs = SSM_DI // SSM_GROUPS
    y = rmsnorm((y * jax.nn.silu(z.astype(jnp.float32))).reshape(B, T, SSM_GROUPS, gs),
                g_norm.reshape(SSM_GROUPS, gs)).reshape(B, T, SSM_DI)
    return y.astype(h.dtype) @ w_out


def peer_ffn(h, w_q, sub_keys, u_tab, v_tab):
    B, T, D = h.shape
    tokens = h.reshape(-1, PEER_TOKEN_BLOCK, D)

    def block(xb):
        M = xb.shape[0]
        q = (xb @ w_q).reshape(M, PEER_HEADS, 2, PEER_HALF)
        s = jnp.einsum('mhcd,hckd->mhck', q, sub_keys).astype(jnp.float32)
        sv, si = lax.top_k(s, PEER_TOPK)
        cand_s = (sv[:, :, 0, :, None] + sv[:, :, 1, None, :]).reshape(M, PEER_HEADS, PEER_TOPK * PEER_TOPK)
        cand_i = (si[:, :, 0, :, None] * PEER_NKEYS + si[:, :, 1, None, :]).reshape(M, PEER_HEADS, PEER_TOPK * PEER_TOPK)
        top_s, pos = lax.top_k(cand_s, PEER_TOPK)
        idx = jnp.take_along_axis(cand_i, pos, axis=-1)
        gates = jax.nn.softmax(top_s, axis=-1)
        u = jnp.take(u_tab, idx, axis=0)
        v = jnp.take(v_tab, idx, axis=0)
        act = jax.nn.gelu(jnp.einsum('md,mhkd->mhk', xb, u).astype(jnp.float32), approximate=False) * gates
        return jnp.einsum('mhk,mhkd->md', act.astype(xb.dtype), v)

    return lax.map(block, tokens).reshape(B, T, D)


def setup_inputs(seed: int = 0) -> dict:
    key = jax.random.key(seed)
    k = jax.random.split(key, 24)
    f32 = jnp.float32

    def nrm(i, shape, scale):
        return jax.random.normal(k[i], shape, f32) * scale

    dsc = D_MODEL ** -0.5
    dt0 = jnp.exp(jax.random.uniform(k[15], (N_SSM, SSM_HEADS), f32, math.log(1e-3), math.log(1e-1)))
    return {
        'x': nrm(0, (BATCH, SEQ, D_MODEL), 1.0),
        'norm_mix': 1.0 + nrm(1, (DEPTH, D_MODEL), 0.02),
        'norm_ffn': 1.0 + nrm(2, (DEPTH, D_MODEL), 0.02),
        'norm_final': 1.0 + nrm(3, (D_MODEL,), 0.02),
        'hgrn_lb_logits': nrm(4, (DEPTH, HG_HK), 0.5),
        'hgrn_w_in': nrm(5, (N_HGRN, D_MODEL, 2 * HG_HK + 2 * HG_HV), dsc),
        'hgrn_gnorm': 1.0 + nrm(6, (N_HGRN, HG_DV), 0.02),
        'hgrn_w_out': nrm(7, (N_HGRN, HG_HV, D_MODEL), HG_HV ** -0.5),
        'fox_w_in': nrm(8, (N_FOX, D_MODEL, 3 * D_MODEL + FOX_HEADS), dsc),
        'fox_b_f': jax.random.uniform(k[9], (N_FOX, FOX_HEADS), f32, 1.0, 6.0),
        'fox_w_out': nrm(10, (N_FOX, D_MODEL, D_MODEL), dsc),
        'ssm_w_in': nrm(11, (N_SSM, D_MODEL, SSM_IN), dsc),
        'ssm_conv_w': nrm(12, (N_SSM, SSM_CONV, SSM_CONV_DIM), SSM_CONV ** -0.5),
        'ssm_conv_b': nrm(13, (N_SSM, SSM_CONV_DIM), 0.02),
        'ssm_dt_bias': dt0 + jnp.log(-jnp.expm1(-dt0)),
        'ssm_a_log': jnp.log(jax.random.uniform(k[14], (N_SSM, SSM_HEADS), f32, 1.0, 16.0)),
        'ssm_d': 1.0 + nrm(16, (N_SSM, SSM_HEADS), 0.1),
        'ssm_gnorm': 1.0 + nrm(17, (N_SSM, SSM_DI), 0.02),
        'ssm_w_out': nrm(18, (N_SSM, SSM_DI, D_MODEL), SSM_DI ** -0.5),
        'peer_w_q': nrm(19, (DEPTH, D_MODEL, PEER_HEADS * PEER_DKEY), dsc),
        'peer_keys': nrm(20, (DEPTH, PEER_HEADS, 2, PEER_NKEYS, PEER_HALF), PEER_HALF ** -0.5),
        'peer_u': nrm(21, (DEPTH, PEER_EXPERTS, D_MODEL), dsc),
        'peer_v': nrm(22, (DEPTH, PEER_EXPERTS, D_MODEL), 0.05),
    }


def reference(x, norm_mix, norm_ffn, norm_final, hgrn_lb_logits, hgrn_w_in, hgrn_gnorm, hgrn_w_out,
              fox_w_in, fox_b_f, fox_w_out, ssm_w_in, ssm_conv_w, ssm_conv_b, ssm_dt_bias, ssm_a_log,
              ssm_d, ssm_gnorm, ssm_w_out, peer_w_q, peer_keys, peer_u, peer_v):
    lb_all = hgrn_lower_bounds(hgrn_lb_logits)
    h = x
    for i in range(DEPTH):
        kind, j = i % N_MIXERS, i // N_MIXERS
        hn = rmsnorm(h, norm_mix[i])
        if kind == 0:
            mix = hgrn2_mixer(hn, hgrn_w_in[j], hgrn_gnorm[j], hgrn_w_out[j], lb_all[i])
        elif kind == 1:
            mix = fox_mixer(hn, fox_w_in[j], fox_b_f[j], fox_w_out[j])
        else:
            mix = ssd_mixer(hn, ssm_w_in[j], ssm_conv_w[j], ssm_conv_b[j], ssm_dt_bias[j], ssm_a_log[j],
                            ssm_d[j], ssm_gnorm[j], ssm_w_out[j])
        h = h + mix.astype(h.dtype)
        ffn = peer_ffn(rmsnorm(h, norm_ffn[i]), peer_w_q[i], peer_keys[i], peer_u[i], peer_v[i])
        h = h + ffn.astype(h.dtype)
    return rmsnorm(h, norm_final)
```

```python
import functools
import math

import numpy as np
import jax
import jax.numpy as jnp
from jax import lax
from jax.experimental import pallas as pl
from jax.experimental.pallas import tpu as pltpu

F32 = jnp.float32
BF16 = jnp.bfloat16

NORM_EPS = 1e-6
LANES = 128
VMEM_LIMIT = 56 * 1024 * 1024

D_MODEL = 1024
HG_HEADS = 8
HG_DK = 128
GLA_CHUNK = 128
GLA_SUB = 16
FOX_HEADS = 16
FOX_DH = 64
SSM_DI = 2048
SSM_HEADS = 32
SSM_HEADDIM = 64
SSM_GROUPS = 8
SSM_STATE = 128
SSM_CONV = 4
SSM_CHUNK = 128
PEER_HEADS = 8
PEER_NKEYS = 128
PEER_TOPK = 16
PEER_HALF = 128

_NT = (((1,), (1,)), ((), ()))
_TN = (((0,), (0,)), ((), ()))


def _cparams(*sem):
    return pltpu.CompilerParams(dimension_semantics=sem, vmem_limit_bytes=VMEM_LIMIT)


def _split3(x):
    hi = x.astype(BF16)
    r = x - hi.astype(F32)
    mid = r.astype(BF16)
    lo = (r - mid.astype(F32)).astype(BF16)
    return hi, mid, lo


def _dot01(m01, x):
    hi, mid, lo = _split3(x)
    d = lambda b: jnp.dot(m01, b, preferred_element_type=F32)
    return d(hi) + d(mid) + d(lo)


def _dot01_r(x, m01):
    hi, mid, lo = _split3(x)
    d = lambda a: jnp.dot(a, m01, preferred_element_type=F32)
    return d(hi) + d(mid) + d(lo)


def _sigmoid(x):
    return 1.0 / (1.0 + jnp.exp(-x))


def _silu(x):
    return x * _sigmoid(x)


def _rms(x, g):
    ms = jnp.mean(x * x, axis=-1, keepdims=True)
    return x * lax.rsqrt(ms + NORM_EPS) * g


def _rms_proj_kernel(x_ref, g_ref, w_ref, o_ref, xn_ref):
    @pl.when(pl.program_id(1) == 0)
    def _():
        xn_ref[...] = _rms(x_ref[...], g_ref[...]).astype(BF16)

    o_ref[...] = jnp.dot(xn_ref[...], w_ref[...], preferred_element_type=F32).astype(o_ref.dtype)


def _rms_proj(x2d, g, w_bf16, out_dtype, tm, tn):
    n, d = x2d.shape
    f = w_bf16.shape[1]
    assert n % tm == 0 and f % tn == 0
    return pl.pallas_call(
        _rms_proj_kernel,
        grid=(n // tm, f // tn),
        in_specs=[pl.BlockSpec((tm, d), lambda i, j: (i, 0)),
                  pl.BlockSpec((1, d), lambda i, j: (0, 0)),
                  pl.BlockSpec((d, tn), lambda i, j: (0, j))],
        out_specs=pl.BlockSpec((tm, tn), lambda i, j: (i, j)),
        out_shape=jax.ShapeDtypeStruct((n, f), out_dtype),
        scratch_shapes=[pltpu.VMEM((tm, d), BF16)],
        compiler_params=_cparams("parallel", "arbitrary"),
        name="rms_proj",
    )(x2d, g.reshape(1, d), w_bf16)


def _mm_res_kernel(x_ref, w_ref, r_ref, o_ref):
    o_ref[...] = r_ref[...] + jnp.dot(x_ref[...], w_ref[...], preferred_element_type=F32)


def _mm_res(x2d_bf16, w_bf16, res2d, tm):
    n, k = x2d_bf16.shape
    d = w_bf16.shape[1]
    return pl.pallas_call(
        _mm_res_kernel,
        grid=(n // tm,),
        in_specs=[pl.BlockSpec((tm, k), lambda i: (i, 0)),
                  pl.BlockSpec((k, d), lambda i: (0, 0)),
                  pl.BlockSpec((tm, d), lambda i: (i, 0))],
        out_specs=pl.BlockSpec((tm, d), lambda i: (i, 0)),
        out_shape=jax.ShapeDtypeStruct((n, d), F32),
        compiler_params=_cparams("parallel"),
        name="mm_res",
    )(x2d_bf16, w_bf16, res2d)


def _final_norm_kernel(x_ref, g_ref, o_ref):
    o_ref[...] = _rms(x_ref[...], g_ref[...])


def _final_norm(x2d, g, tm):
    n, d = x2d.shape
    return pl.pallas_call(
        _final_norm_kernel,
        grid=(n // tm,),
        in_specs=[pl.BlockSpec((tm, d), lambda i: (i, 0)),
                  pl.BlockSpec((1, d), lambda i: (0, 0))],
        out_specs=pl.BlockSpec((tm, d), lambda i: (i, 0)),
        out_shape=jax.ShapeDtypeStruct((n, d), F32),
        compiler_params=_cparams("parallel"),
        name="final_norm",
    )(x2d, g.reshape(1, d))


def _gla_tables(c, sub):
    t = np.arange(c)[:, None]
    j = np.arange(c)[None, :]
    mats = [(j <= t).astype(np.float32),
            (j > t).astype(np.float32)]
    masks = []
    bk = c // 2
    while bk >= sub:
        blk = t // bk
        odd = (blk % 2) == 1
        mats.append((odd & (j >= blk * bk) & (j <= t)).astype(np.float32))
        mats.append(((~odd) & (j > t) & (j < (blk + 1) * bk)).astype(np.float32))
        masks.append((odd & ((j // bk) == blk - 1)).astype(np.float32))
        bk //= 2
    m0 = (t // sub) * sub
    dq = ((j >= m0) & (j <= t)).astype(np.float32)
    mats += [dq, -dq]
    masks.append((((t // sub) == (j // sub)) & (j <= t)).astype(np.float32))
    return np.concatenate(mats, axis=0), np.stack(masks, axis=0)


def _hgrn_kernel(proj_ref, h_ref, lbl_ref, gn_ref, wout_ref, dst_ref, msk_ref, o_ref,
                 st_ref, y_ref, *, layer, depth, c, nlev):
    @pl.when(pl.program_id(1) == 0)
    def _():
        st_ref[...] = jnp.zeros_like(st_ref)

    rows = [lbl_ref[j:j + 1, :] for j in range(depth)]
    mx = functools.reduce(jnp.maximum, rows)
    es = [jnp.exp(r - mx) for r in rows]
    tot = functools.reduce(lambda a, b: a + b, es)
    lb = jnp.zeros_like(mx)
    for j in range(1, layer + 1):
        lb = lb + es[j] / tot

    d = HG_HEADS * HG_DK
    q = proj_ref[0, :, 0:d] * (HG_DK ** -0.5)
    f = lb + (1.0 - lb) * _sigmoid(proj_ref[0, :, d:2 * d])
    k = 1.0 - f
    ex = jnp.exp(_dot01(dst_ref[...], jnp.log(f)))
    q_dec = (q * ex[0:c]).astype(BF16)
    k_dec = (k * ex[c:2 * c]).astype(BF16)
    dec_last = ex[c - 1:c]
    qs, ks = [], []
    for l in range(nlev):
        qs.append((q * ex[(2 + 2 * l) * c:(3 + 2 * l) * c]).astype(BF16))
        ks.append((k * ex[(3 + 2 * l) * c:(4 + 2 * l) * c]).astype(BF16))
    v = proj_ref[0, :, 2 * d:3 * d].astype(BF16)
    gate = proj_ref[0, :, 3 * d:4 * d]
    gn = gn_ref[...]
    for hd in range(HG_HEADS):
        sl = slice(hd * HG_DK, (hd + 1) * HG_DK)
        a = jnp.zeros((c, c), F32)
        for l in range(nlev):
            a = a + lax.dot_general(qs[l][:, sl], ks[l][:, sl], _NT,
                                    preferred_element_type=F32) * msk_ref[l]
        st = st_ref[hd]
        o = (jnp.dot(a.astype(BF16), v[:, sl], preferred_element_type=F32)
             + lax.dot_general(q_dec[:, sl], st.astype(BF16), _NT, preferred_element_type=F32))
        st_ref[hd] = st * dec_last[:, sl] + lax.dot_general(
            v[:, sl], k_dec[:, sl], _TN, preferred_element_type=F32)
        y_ref[:, sl] = (_rms(o, gn) * _silu(gate[:, sl])).astype(BF16)
    o_ref[0] = h_ref[0] + jnp.dot(y_ref[...], wout_ref[...], preferred_element_type=F32)


def _hgrn_layer(h, proj, lb_logits, g_norm, w_out_bf16, layer):
    b, t, d = h.shape
    c = GLA_CHUNK
    dst, msk = _gla_tables(c, GLA_SUB)
    nlev = msk.shape[0]
    depth = lb_logits.shape[0]
    kern = functools.partial(_hgrn_kernel, layer=layer, depth=depth, c=c, nlev=nlev)
    return pl.pallas_call(
        kern,
        grid=(b, t // c),
        in_specs=[pl.BlockSpec((1, c, 4 * d), lambda i, j: (i, j, 0)),
                  pl.BlockSpec((1, c, d), lambda i, j: (i, j, 0)),
                  pl.BlockSpec((depth, d), lambda i, j: (0, 0)),
                  pl.BlockSpec((1, HG_DK), lambda i, j: (0, 0)),
                  pl.BlockSpec((d, d), lambda i, j: (0, 0)),
                  pl.BlockSpec(dst.shape, lambda i, j: (0, 0)),
                  pl.BlockSpec(msk.shape, lambda i, j: (0, 0, 0))],
        out_specs=pl.BlockSpec((1, c, d), lambda i, j: (i, j, 0)),
        out_shape=jax.ShapeDtypeStruct((b, t, d), F32),
        scratch_shapes=[pltpu.VMEM((HG_HEADS, HG_DK, HG_DK), F32),
                        pltpu.VMEM((c, d), BF16)],
        compiler_params=_cparams("parallel", "arbitrary"),
        name="hgrn_gla",
    )(proj, h, lb_logits, g_norm.reshape(1, HG_DK), w_out_bf16,
      jnp.asarray(dst, BF16), jnp.asarray(msk, F32))


def _log_sigmoid(x):
    return jnp.minimum(x, 0.0) - jnp.log1p(jnp.exp(-jnp.abs(x)))


def _fox_cum_kernel(fl_ref, bf_ref, utri_ref, ones_ref, o_ref, carry_ref, *, tc):
    @pl.when(pl.program_id(1) == 0)
    def _():
        carry_ref[...] = jnp.zeros_like(carry_ref)

    lf_t = _log_sigmoid(fl_ref[0] + bf_ref[...]).T
    carry = carry_ref[...]
    o_ref[0] = _dot01_r(lf_t, utri_ref[...]) + jnp.tile(carry, (1, tc // LANES))
    carry_ref[...] = carry + _dot01_r(lf_t, ones_ref[...])


def _fox_cumsum(f_logit, b_f_pad, tc):
    b, t, _ = f_logit.shape
    utri = np.triu(np.ones((tc, tc), np.float32))
    return pl.pallas_call(
        functools.partial(_fox_cum_kernel, tc=tc),
        grid=(b, t // tc),
        in_specs=[pl.BlockSpec((1, tc, LANES), lambda i, j: (i, j, 0)),
                  pl.BlockSpec((1, LANES), lambda i, j: (0, 0)),
                  pl.BlockSpec((tc, tc), lambda i, j: (0, 0)),
                  pl.BlockSpec((tc, LANES), lambda i, j: (0, 0))],
        out_specs=pl.BlockSpec((1, LANES, tc), lambda i, j: (i, 0, j)),
        out_shape=jax.ShapeDtypeStruct((b, LANES, t), F32),
        scratch_shapes=[pltpu.VMEM((LANES, LANES), F32)],
        compiler_params=_cparams("parallel", "arbitrary"),
        name="fox_cumsum",
    )(f_logit, b_f_pad, jnp.asarray(utri, BF16), jnp.ones((tc, LANES), BF16))


def _fox_flash_kernel(q_ref, k_ref, v_ref, c_ref, o_ref, m_ref, l_ref, acc_ref, *, tq, tk):
    qi = pl.program_id(2)
    ki = pl.program_id(3)

    @pl.when(ki == 0)
    def _():
        m_ref[...] = jnp.full_like(m_ref, -1e30)
        l_ref[...] = jnp.zeros_like(l_ref)
        acc_ref[...] = jnp.zeros_like(acc_ref)

    lane = lax.broadcasted_iota(jnp.int32, (1, LANES), 1)

    def body(masked):
        q = q_ref[0] * (FOX_DH ** -0.5)
        k = k_ref[0]
        v = v_ref[0]
        if masked:
            row = lax.broadcasted_iota(jnp.int32, (tq, tk), 0)
            col = lax.broadcasted_iota(jnp.int32, (tq, tk), 1)
            causal = row >= col
        for hh in range(2):
            qh = jnp.where((lane < FOX_DH) == (hh == 0), q, jnp.zeros_like(q))
            s = lax.dot_general(qh, k, _NT, preferred_element_type=F32)
            s = s - c_ref[0, 0, hh:hh + 1, :]
            if masked:
                s = jnp.where(causal, s, -1e30)
            m_prev = m_ref[hh]
            m_new = jnp.maximum(m_prev, jnp.max(s, axis=-1, keepdims=True))
            alpha = jnp.exp(m_prev - m_new)
            p = jnp.exp(s - m_new)
            l_ref[hh] = alpha * l_ref[hh] + jnp.sum(p, axis=-1, keepdims=True)
            acc_ref[hh] = alpha * acc_ref[hh] + jnp.dot(p.astype(BF16), v,
                                                        preferred_element_type=F32)
            m_ref[hh] = m_new

    @pl.when(ki < qi)
    def _():
        body(False)

    @pl.when(ki == qi)
    def _():
        body(True)
        o0 = acc_ref[0] / l_ref[0]
        o1 = acc_ref[1] / l_ref[1]
        o_ref[0] = jnp.where(lane < FOX_DH, o0, o1).astype(o_ref.dtype)


def _fox_flash(qkv, cpad, tq):
    b, t, _ = qkv.shape
    npair = FOX_HEADS // 2
    tk = tq
    kv_idx = lambda i, hp, qi, ki: jnp.minimum(ki, qi)
    return pl.pallas_call(
        functools.partial(_fox_flash_kernel, tq=tq, tk=tk),
        grid=(b, npair, t // tq, t // tk),
        in_specs=[pl.BlockSpec((1, tq, LANES), lambda i, hp, qi, ki: (i, qi, hp)),
                  pl.BlockSpec((1, tk, LANES), lambda i, hp, qi, ki: (i, kv_idx(i, hp, qi, ki), npair + hp)),
                  pl.BlockSpec((1, tk, LANES), lambda i, hp, qi, ki: (i, kv_idx(i, hp, qi, ki), 2 * npair + hp)),
                  pl.BlockSpec((1, 1, 8, tk), lambda i, hp, qi, ki: (i, hp, 0, kv_idx(i, hp, qi, ki)))],
        out_specs=pl.BlockSpec((1, tq, LANES), lambda i, hp, qi, ki: (i, qi, hp)),
        out_shape=jax.ShapeDtypeStruct((b, t, D_MODEL), BF16),
        scratch_shapes=[pltpu.VMEM((2, tq, 1), F32),
                        pltpu.VMEM((2, tq, 1), F32),
                        pltpu.VMEM((2, tq, LANES), F32)],
        compiler_params=_cparams("parallel", "parallel", "parallel", "arbitrary"),
        name="fox_flash",
    )(qkv, qkv, qkv, cpad)


def _fox_layer(h, g_mix, w_in, b_f, w_out):
    b, t, d = h.shape
    n = b * t
    h2 = h.reshape(n, d)
    w_qkv = w_in[:, :3 * d].astype(BF16)
    w_f = jnp.pad(w_in[:, 3 * d:], ((0, 0), (0, LANES - FOX_HEADS))).astype(BF16)
    qkv = _rms_proj(h2, g_mix, w_qkv, BF16, 512, 1024).reshape(b, t, 3 * d)
    fl = _rms_proj(h2, g_mix, w_f, F32, 512, LANES).reshape(b, t, LANES)
    b_f_pad = jnp.pad(b_f, (0, LANES - FOX_HEADS)).reshape(1, LANES)
    c_t = _fox_cumsum(fl, b_f_pad, min(512, t))
    cpad = jnp.pad(c_t[:, :FOX_HEADS].reshape(b, FOX_HEADS // 2, 2, t),
                   ((0, 0), (0, 0), (0, 6), (0, 0)))
    o = _fox_flash(qkv, cpad, min(512, t))
    return _mm_res(o.reshape(n, d), w_out.astype(BF16), h2, 512).reshape(b, t, d)


def _softplus(x):
    return jnp.maximum(x, 0.0) + jnp.log1p(jnp.exp(-jnp.abs(x)))


def _ssd_kernel(p_ref, h_ref, cw_ref, cb_ref, dtb_ref, alog_ref, dex_ref, gn_ref, wout_ref,
                ltri_ref, eexp_ref, o_ref, buf_ref, st_ref, y_ref, *, c):
    di = SSM_DI
    gn_w = SSM_GROUPS * SSM_STATE
    gw = di // SSM_GROUPS

    @pl.when(pl.program_id(1) == 0)
    def _():
        buf_ref[0:8, :] = jnp.zeros((8, buf_ref.shape[1]), F32)
        st_ref[...] = jnp.zeros_like(st_ref)

    buf_ref[8:8 + c, :] = p_ref[0, :, di:di + di + 2 * gn_w]
    conv = cb_ref[...]
    for kk in range(SSM_CONV):
        conv = conv + cw_ref[kk:kk + 1, :] * buf_ref[5 + kk:5 + kk + c, :]
    buf_ref[0:8, :] = buf_ref[c:c + 8, :]
    xbc = _silu(conv)
    xs = xbc[:, :di]
    bm = xbc[:, di:di + gn_w]
    cm = xbc[:, di + gn_w:]

    dt = _softplus(p_ref[0, :, 2 * di + 2 * gn_w:] + dtb_ref[...])
    da = dt * (-jnp.exp(alog_ref[...]))
    cum = _dot01(ltri_ref[...], da)
    cum_t = cum.T
    ex = _dot01_r(jnp.concatenate([cum, dt], axis=0), eexp_ref[...])
    cum_e = ex[:c]
    dt_e = ex[c:]
    last = cum_e[c - 1:c]
    expcum = jnp.exp(cum_e)
    declast = jnp.exp(last)
    xdt = xs * dt_e
    xdt_b = xdt.astype(BF16)
    xdec_b = (xdt * jnp.exp(last - cum_e)).astype(BF16)

    row = lax.broadcasted_iota(jnp.int32, (c, c), 0)
    col = lax.broadcasted_iota(jnp.int32, (c, c), 1)
    tril = row >= col
    lane = lax.broadcasted_iota(jnp.int32, (1, LANES), 1)
    for g in range(SSM_GROUPS):
        cg = cm[:, g * SSM_STATE:(g + 1) * SSM_STATE].astype(BF16)
        bg32 = bm[:, g * SSM_STATE:(g + 1) * SSM_STATE]
        cb = lax.dot_general(cg, bg32.astype(BF16), _NT, preferred_element_type=F32)
        st = st_ref[g]
        y_inter = (jnp.dot(cg, st.astype(BF16), preferred_element_type=F32)
                   * expcum[:, g * gw:(g + 1) * gw])
        for pr in range(2):
            base = g * gw + pr * LANES
            xpair = xdt_b[:, base:base + LANES]
            outs = []
            for hh in range(2):
                hidx = g * 4 + pr * 2 + hh
                diff = cum[:, hidx:hidx + 1] - cum_t[hidx:hidx + 1, :]
                lmat = jnp.exp(jnp.where(tril, diff, -1e30))
                outs.append(jnp.dot((cb * lmat).astype(BF16), xpair, preferred_element_type=F32))
            y_intra = jnp.where(lane < SSM_HEADDIM, outs[0], outs[1])
            y_ref[:, base:base + LANES] = (y_intra + y_inter[:, pr * LANES:(pr + 1) * LANES]
                                           + xs[:, base:base + LANES] * dex_ref[:, base:base + LANES])
        st_ref[g] = st * declast[:, g * gw:(g + 1) * gw] + jnp.dot(
            bg32.T.astype(BF16), xdec_b[:, g * gw:(g + 1) * gw], preferred_element_type=F32)

    y = y_ref[...] * _silu(p_ref[0, :, 0:di])
    for g in range(SSM_GROUPS):
        sl = slice(g * gw, (g + 1) * gw)
        y_ref[:, sl] = _rms(y[:, sl], gn_ref[:, sl])
    o_ref[0] = h_ref[0] + jnp.dot(y_ref[...].astype(BF16), wout_ref[...], preferred_element_type=F32)


def _ssd_layer(h, g_mix, w_in, conv_w, conv_b, dt_bias, a_log, d_skip, g_norm, w_out):
    b, t, d = h.shape
    n = b * t
    c = min(SSM_CHUNK, t)
    di = SSM_DI
    cdim = di + 2 * SSM_GROUPS * SSM_STATE
    pad_h = LANES - SSM_HEADS
    w_all = jnp.pad(w_in, ((0, 0), (0, pad_h))).astype(BF16)
    fdim = w_all.shape[1]
    proj = _rms_proj(h.reshape(n, d), g_mix, w_all, F32, 512, 896).reshape(b, t, fdim)
    eexp = np.zeros((LANES, di), np.float32)
    for hd in range(SSM_HEADS):
        eexp[hd, hd * SSM_HEADDIM:(hd + 1) * SSM_HEADDIM] = 1.0
    ltri = np.tril(np.ones((c, c), np.float32))
    full = lambda shape: pl.BlockSpec(shape, lambda i, j: tuple(0 for _ in shape))
    return pl.pallas_call(
        functools.partial(_ssd_kernel, c=c),
        grid=(b, t // c),
        in_specs=[pl.BlockSpec((1, c, fdim), lambda i, j: (i, j, 0)),
                  pl.BlockSpec((1, c, d), lambda i, j: (i, j, 0)),
                  full((SSM_CONV, cdim)), full((1, cdim)), full((1, LANES)), full((1, LANES)),
                  full((1, di)), full((1, di)), full((di, d)), full((c, c)), full((LANES, di))],
        out_specs=pl.BlockSpec((1, c, d), lambda i, j: (i, j, 0)),
        out_shape=jax.ShapeDtypeStruct((b, t, d), F32),
        scratch_shapes=[pltpu.VMEM((c + 8, cdim), F32),
                        pltpu.VMEM((SSM_GROUPS, SSM_STATE, di // SSM_GROUPS), F32),
                        pltpu.VMEM((c, di), F32)],
        compiler_params=_cparams("parallel", "arbitrary"),
        name="ssd_scan",
    )(proj, h, conv_w, conv_b.reshape(1, cdim),
      jnp.pad(dt_bias, (0, pad_h)).reshape(1, LANES), jnp.pad(a_log, (0, pad_h)).reshape(1, LANES),
      jnp.repeat(d_skip, SSM_HEADDIM).reshape(1, di), g_norm.reshape(1, di), w_out.astype(BF16),
      jnp.asarray(ltri, BF16), jnp.asarray(eexp, BF16))


def _top16(s, nrows):
    iota = lax.broadcasted_iota(jnp.int32, (nrows, LANES), 0)
    rank = jnp.full((nrows, LANES), PEER_TOPK, jnp.int32)
    vals = []
    for it in range(PEER_TOPK):
        m = jnp.max(s, axis=0, keepdims=True)
        idx = jnp.min(jnp.where(s == m, iota, nrows), axis=0, keepdims=True)
        sel = iota == idx
        rank = jnp.where(sel, it, rank)
        s = jnp.where(sel, -jnp.inf, s)
        vals.append(m)
    return vals, rank


def _peer_route_kernel(h_ref, g_ref, wq_ref, keys_ref, xn_ref, r2_ref, e2_ref, na_ref, ca_ref,
                       xs_ref, q_ref, *, mt):
    @pl.when(pl.program_id(1) == 0)
    def _():
        xn = _rms(h_ref[...], g_ref[...]).astype(BF16)
        xs_ref[...] = xn
        xn_ref[...] = xn

    q_ref[...] = jnp.dot(xs_ref[...], wq_ref[...], preferred_element_type=F32).astype(BF16)
    k1 = keys_ref[0, 0]
    k2 = keys_ref[0, 1]

    def sub_tile(sb, carry):
        qs = q_ref[pl.ds(pl.multiple_of(sb * LANES, LANES), LANES), :]
        s1 = lax.dot_general(k1, qs[:, :PEER_HALF], _NT, preferred_element_type=F32)
        s2 = lax.dot_general(k2, qs[:, PEER_HALF:], _NT, preferred_element_type=F32)
        v1, rank1 = _top16(s1, PEER_NKEYS)
        v2, rank2 = _top16(s2, PEER_NKEYS)
        iota16 = lax.broadcasted_iota(jnp.int32, (PEER_TOPK, LANES), 0)
        sv2 = jnp.zeros((PEER_TOPK, LANES), F32)
        for it in range(PEER_TOPK):
            sv2 = jnp.where(iota16 == it, v2[it], sv2)
        cand = jnp.concatenate([v1[i] + sv2 for i in range(PEER_TOPK)], axis=0)
        _, rank_c = _top16(cand, PEER_TOPK * PEER_TOPK)
        sel = rank_c < PEER_TOPK
        e2r = jnp.exp(sv2 - v2[0])
        n_rows, w_rows = [], []
        z = jnp.zeros((1, LANES), F32)
        for i in range(PEER_TOPK):
            sel_i = sel[i * PEER_TOPK:(i + 1) * PEER_TOPK]
            n_rows.append(jnp.sum(sel_i.astype(F32), axis=0, keepdims=True))
            e1 = jnp.exp(v1[i] - v1[0])
            w_rows.append(e1)
            z = z + e1 * jnp.sum(jnp.where(sel_i, e2r, 0.0), axis=0, keepdims=True)
        zinv = 1.0 / z
        na = jnp.zeros((PEER_NKEYS, LANES), F32)
        ca = jnp.zeros((PEER_NKEYS, LANES), F32)
        for i in range(PEER_TOPK):
            hit = rank1 == i
            na = jnp.where(hit, n_rows[i], na)
            ca = jnp.where(hit, w_rows[i] * zinv, ca)
        r2_ref[sb, 0] = rank2.astype(F32)
        e2_ref[sb, 0] = jnp.exp(s2 - v2[0])
        na_ref[sb, 0] = na
        ca_ref[sb, 0] = ca
        return carry

    lax.fori_loop(0, mt // LANES, sub_tile, 0)


def _gelu(x):
    return 0.5 * x * (1.0 + lax.erf(x * (0.5 ** 0.5)))


def _peer_dense_kernel(xn_ref, u_ref, vt_ref, r2_ref, e2_ref, na_ref, ca_ref, h_ref, o_ref,
                       acc_ref, hid_ref, *, mt, eb):
    e = pl.program_id(1)

    @pl.when(e == 0)
    def _():
        acc_ref[...] = jnp.zeros_like(acc_ref)

    xu = lax.dot_general(u_ref[...], xn_ref[...], _NT, preferred_element_type=F32)
    for aa in range(eb // LANES):
        a = e * (eb // LANES) + aa
        for sb in range(mt // LANES):
            act = _gelu(xu[aa * LANES:(aa + 1) * LANES, sb * LANES:(sb + 1) * LANES])
            gsum = jnp.zeros((LANES, LANES), F32)
            for hd in range(PEER_HEADS):
                n_row = na_ref[sb, hd, pl.ds(a, 1), :]
                c_row = ca_ref[sb, hd, pl.ds(a, 1), :]
                gsum = gsum + jnp.where(r2_ref[sb, hd] < n_row, e2_ref[sb, hd], 0.0) * c_row
            hid_ref[aa * LANES:(aa + 1) * LANES, sb * LANES:(sb + 1) * LANES] = (act * gsum).astype(BF16)
    acc_ref[...] += jnp.dot(vt_ref[...], hid_ref[...], preferred_element_type=F32)

    @pl.when(e == pl.num_programs(1) - 1)
    def _():
        o_ref[...] = h_ref[...] + acc_ref[...].T


def _peer_layer(h2, g_ffn, w_q, keys, u_tab, v_tab, mt, eb):
    n, d = h2.shape
    ne = u_tab.shape[0]
    nsub = mt // LANES
    rshape = jax.ShapeDtypeStruct((n // LANES, PEER_HEADS, PEER_NKEYS, LANES), F32)
    rspec = pl.BlockSpec((nsub, 1, PEER_NKEYS, LANES), lambda i, hd: (i, hd, 0, 0))
    xn, r2, e2, na, ca = pl.pallas_call(
        functools.partial(_peer_route_kernel, mt=mt),
        grid=(n // mt, PEER_HEADS),
        in_specs=[pl.BlockSpec((mt, d), lambda i, hd: (i, 0)),
                  pl.BlockSpec((1, d), lambda i, hd: (0, 0)),
                  pl.BlockSpec((d, 2 * PEER_HALF), lambda i, hd: (0, hd)),
                  pl.BlockSpec((1, 2, PEER_NKEYS, PEER_HALF), lambda i, hd: (hd, 0, 0, 0))],
        out_specs=[pl.BlockSpec((mt, d), lambda i, hd: (i, 0)), rspec, rspec, rspec, rspec],
        out_shape=[jax.ShapeDtypeStruct((n, d), BF16), rshape, rshape, rshape, rshape],
        scratch_shapes=[pltpu.VMEM((mt, d), BF16), pltpu.VMEM((mt, 2 * PEER_HALF), BF16)],
        compiler_params=_cparams("parallel", "arbitrary"),
        name="peer_route",
    )(h2, g_ffn.reshape(1, d), w_q.astype(BF16), keys.astype(BF16))

    dspec = pl.BlockSpec((nsub, PEER_HEADS, PEER_NKEYS, LANES), lambda i, e: (i, 0, 0, 0))
    return pl.pallas_call(
        functools.partial(_peer_dense_kernel, mt=mt, eb=eb),
        grid=(n // mt, ne // eb),
        in_specs=[pl.BlockSpec((mt, d), lambda i, e: (i, 0)),
                  pl.BlockSpec((eb, d), lambda i, e: (e, 0)),
                  pl.BlockSpec((d, eb), lambda i, e: (0, e)),
                  dspec, dspec, dspec, dspec,
                  pl.BlockSpec((mt, d), lambda i, e: (i, 0))],
        out_specs=pl.BlockSpec((mt, d), lambda i, e: (i, 0)),
        out_shape=jax.ShapeDtypeStruct((n, d), F32),
        scratch_shapes=[pltpu.VMEM((d, mt), F32), pltpu.VMEM((eb, mt), BF16)],
        compiler_params=_cparams("parallel", "arbitrary"),
        name="peer_dense",
    )(xn, u_tab.astype(BF16), v_tab.T.astype(BF16), r2, e2, na, ca, h2)


def kernel(x, norm_mix, norm_ffn, norm_final, hgrn_lb_logits, hgrn_w_in, hgrn_gnorm, hgrn_w_out,
           fox_w_in, fox_b_f, fox_w_out, ssm_w_in, ssm_conv_w, ssm_conv_b, ssm_dt_bias, ssm_a_log,
           ssm_d, ssm_gnorm, ssm_w_out, peer_w_q, peer_keys, peer_u, peer_v):
    b, t, d = x.shape
    n = b * t
    depth = norm_mix.shape[0]
    h = x
    for i in range(depth):
        kind, j = i % 3, i // 3
        if kind == 0:
            proj = _rms_proj(h.reshape(n, d), norm_mix[i], hgrn_w_in[j].astype(BF16), F32, 512, 1024)
            h = _hgrn_layer(h, proj.reshape(b, t, 4 * d), hgrn_lb_logits, hgrn_gnorm[j],
                            hgrn_w_out[j].astype(BF16), i)
        elif kind == 1:
            h = _fox_layer(h, norm_mix[i], fox_w_in[j], fox_b_f[j], fox_w_out[j])
        else:
            h = _ssd_layer(h, norm_mix[i], ssm_w_in[j], ssm_conv_w[j], ssm_conv_b[j], ssm_dt_bias[j],
                           ssm_a_log[j], ssm_d[j], ssm_gnorm[j], ssm_w_out[j])
        h = _peer_layer(h.reshape(n, d), norm_ffn[i], peer_w_q[i], peer_keys[i], peer_u[i],
                        peer_v[i], min(512, n), 512).reshape(b, t, d)
    return _final_norm(h.reshape(n, d), norm_final, 512).reshape(b, t, d)
```

```python
import functools
import math

import numpy as np
import jax
import jax.numpy as jnp
from jax import lax
from jax.experimental import pallas as pl
from jax.experimental.pallas import tpu as pltpu

F32 = jnp.float32
BF16 = jnp.bfloat16

NORM_EPS = 1e-6
LANES = 128
BF16_SUBLANES = 16
VMEM_LIMIT = 56 * 1024 * 1024

D_MODEL = 1024
HG_HEADS = 8
HG_DK = 128
GLA_CHUNK = 128
GLA_SUB = 16
FOX_HEADS = 16
FOX_DH = 64
SSM_DI = 2048
SSM_HEADS = 32
SSM_HEADDIM = 64
SSM_GROUPS = 8
SSM_STATE = 128
SSM_CONV = 4
SSM_CHUNK = 128
PEER_HEADS = 8
PEER_NKEYS = 128
PEER_TOPK = 16
PEER_HALF = 128
PEER_CHUNK = 256

_NT = (((1,), (1,)), ((), ()))
_TN = (((0,), (0,)), ((), ()))


def _cparams(*sem):
    return pltpu.CompilerParams(dimension_semantics=sem, vmem_limit_bytes=VMEM_LIMIT)


def _split3(x):
    hi = x.astype(BF16)
    r = x - hi.astype(F32)
    mid = r.astype(BF16)
    lo = (r - mid.astype(F32)).astype(BF16)
    return hi, mid, lo


def _dot01(m01, x):
    hi, mid, lo = _split3(x)
    d = lambda b: jnp.dot(m01, b, preferred_element_type=F32)
    return d(hi) + d(mid) + d(lo)


def _dot01_r(x, m01):
    hi, mid, lo = _split3(x)
    d = lambda a: jnp.dot(a, m01, preferred_element_type=F32)
    return d(hi) + d(mid) + d(lo)


def _sigmoid(x):
    return 1.0 / (1.0 + jnp.exp(-x))


def _silu(x):
    return x * _sigmoid(x)


def _rms(x, g):
    ms = jnp.mean(x * x, axis=-1, keepdims=True)
    return x * lax.rsqrt(ms + NORM_EPS) * g


def _rms_proj_kernel(x_ref, g_ref, w_ref, o_ref, xn_ref):
    @pl.when(pl.program_id(1) == 0)
    def _():
        xn_ref[...] = _rms(x_ref[...], g_ref[...]).astype(BF16)

    o_ref[...] = jnp.dot(xn_ref[...], w_ref[...], preferred_element_type=F32).astype(o_ref.dtype)


def _rms_proj(x2d, g, w_bf16, out_dtype, tm, tn):
    n, d = x2d.shape
    f = w_bf16.shape[1]
    assert n % tm == 0 and f % tn == 0
    return pl.pallas_call(
        _rms_proj_kernel,
        grid=(n // tm, f // tn),
        in_specs=[pl.BlockSpec((tm, d), lambda i, j: (i, 0)),
                  pl.BlockSpec((1, d), lambda i, j: (0, 0)),
                  pl.BlockSpec((d, tn), lambda i, j: (0, j))],
        out_specs=pl.BlockSpec((tm, tn), lambda i, j: (i, j)),
        out_shape=jax.ShapeDtypeStruct((n, f), out_dtype),
        scratch_shapes=[pltpu.VMEM((tm, d), BF16)],
        compiler_params=_cparams("parallel", "arbitrary"),
        name="rms_proj",
    )(x2d, g.reshape(1, d), w_bf16)


def _mm_res_kernel(x_ref, w_ref, r_ref, o_ref):
    o_ref[...] = r_ref[...] + jnp.dot(x_ref[...], w_ref[...], preferred_element_type=F32)


def _mm_res(x2d_bf16, w_bf16, res2d, tm):
    n, k = x2d_bf16.shape
    d = w_bf16.shape[1]
    return pl.pallas_call(
        _mm_res_kernel,
        grid=(n // tm,),
        in_specs=[pl.BlockSpec((tm, k), lambda i: (i, 0)),
                  pl.BlockSpec((k, d), lambda i: (0, 0)),
                  pl.BlockSpec((tm, d), lambda i: (i, 0))],
        out_specs=pl.BlockSpec((tm, d), lambda i: (i, 0)),
        out_shape=jax.ShapeDtypeStruct((n, d), F32),
        compiler_params=_cparams("parallel"),
        name="mm_res",
    )(x2d_bf16, w_bf16, res2d)


def _final_norm_kernel(x_ref, g_ref, o_ref):
    o_ref[...] = _rms(x_ref[...], g_ref[...])


def _final_norm(x2d, g, tm):
    n, d = x2d.shape
    return pl.pallas_call(
        _final_norm_kernel,
        grid=(n // tm,),
        in_specs=[pl.BlockSpec((tm, d), lambda i: (i, 0)),
                  pl.BlockSpec((1, d), lambda i: (0, 0))],
        out_specs=pl.BlockSpec((tm, d), lambda i: (i, 0)),
        out_shape=jax.ShapeDtypeStruct((n, d), F32),
        compiler_params=_cparams("parallel"),
        name="final_norm",
    )(x2d, g.reshape(1, d))


def _gla_tables(c, sub):
    t = np.arange(c)[:, None]
    j = np.arange(c)[None, :]
    mats = [(j <= t).astype(np.float32),
            (j > t).astype(np.float32)]
    masks = []
    bk = c // 2
    while bk >= sub:
        blk = t // bk
        odd = (blk % 2) == 1
        mats.append((odd & (j >= blk * bk) & (j <= t)).astype(np.float32))
        mats.append(((~odd) & (j > t) & (j < (blk + 1) * bk)).astype(np.float32))
        masks.append((odd & ((j // bk) == blk - 1)).astype(np.float32))
        bk //= 2
    m0 = (t // sub) * sub
    dq = ((j >= m0) & (j <= t)).astype(np.float32)
    mats += [dq, -dq]
    masks.append((((t // sub) == (j // sub)) & (j <= t)).astype(np.float32))
    return np.concatenate(mats, axis=0), np.stack(masks, axis=0)


def _hgrn_kernel(proj_ref, h_ref, lbl_ref, gn_ref, wout_ref, dst_ref, msk_ref, o_ref,
                 st_ref, y_ref, *, layer, depth, c, nlev):
    @pl.when(pl.program_id(1) == 0)
    def _():
        st_ref[...] = jnp.zeros_like(st_ref)

    rows = [lbl_ref[j:j + 1, :] for j in range(depth)]
    mx = functools.reduce(jnp.maximum, rows)
    es = [jnp.exp(r - mx) for r in rows]
    tot = functools.reduce(lambda a, b: a + b, es)
    lb = jnp.zeros_like(mx)
    for j in range(1, layer + 1):
        lb = lb + es[j] / tot

    d = HG_HEADS * HG_DK
    q = proj_ref[0, :, 0:d] * (HG_DK ** -0.5)
    f = lb + (1.0 - lb) * _sigmoid(proj_ref[0, :, d:2 * d])
    k = 1.0 - f
    ex = jnp.exp(_dot01(dst_ref[...], jnp.log(f)))
    q_dec = (q * ex[0:c]).astype(BF16)
    k_dec = (k * ex[c:2 * c]).astype(BF16)
    dec_last = ex[c - 1:c]
    qs, ks = [], []
    for l in range(nlev):
        qs.append((q * ex[(2 + 2 * l) * c:(3 + 2 * l) * c]).astype(BF16))
        ks.append((k * ex[(3 + 2 * l) * c:(4 + 2 * l) * c]).astype(BF16))
    v = proj_ref[0, :, 2 * d:3 * d].astype(BF16)
    gate = proj_ref[0, :, 3 * d:4 * d]
    gn = gn_ref[...]
    for hd in range(HG_HEADS):
        sl = slice(hd * HG_DK, (hd + 1) * HG_DK)
        a = jnp.zeros((c, c), F32)
        for l in range(nlev):
            a = a + lax.dot_general(qs[l][:, sl], ks[l][:, sl], _NT,
                                    preferred_element_type=F32) * msk_ref[l]
        st = st_ref[hd]
        o = (jnp.dot(a.astype(BF16), v[:, sl], preferred_element_type=F32)
             + lax.dot_general(q_dec[:, sl], st.astype(BF16), _NT, preferred_element_type=F32))
        st_ref[hd] = st * dec_last[:, sl] + lax.dot_general(
            v[:, sl], k_dec[:, sl], _TN, preferred_element_type=F32)
        y_ref[:, sl] = (_rms(o, gn) * _silu(gate[:, sl])).astype(BF16)
    o_ref[0] = h_ref[0] + jnp.dot(y_ref[...], wout_ref[...], preferred_element_type=F32)


def _hgrn_layer(h, proj, lb_logits, g_norm, w_out_bf16, layer):
    b, t, d = h.shape
    c = GLA_CHUNK
    dst, msk = _gla_tables(c, GLA_SUB)
    nlev = msk.shape[0]
    depth = lb_logits.shape[0]
    kern = functools.partial(_hgrn_kernel, layer=layer, depth=depth, c=c, nlev=nlev)
    return pl.pallas_call(
        kern,
        grid=(b, t // c),
        in_specs=[pl.BlockSpec((1, c, 4 * d), lambda i, j: (i, j, 0)),
                  pl.BlockSpec((1, c, d), lambda i, j: (i, j, 0)),
                  pl.BlockSpec((depth, d), lambda i, j: (0, 0)),
                  pl.BlockSpec((1, HG_DK), lambda i, j: (0, 0)),
                  pl.BlockSpec((d, d), lambda i, j: (0, 0)),
                  pl.BlockSpec(dst.shape, lambda i, j: (0, 0)),
                  pl.BlockSpec(msk.shape, lambda i, j: (0, 0, 0))],
        out_specs=pl.BlockSpec((1, c, d), lambda i, j: (i, j, 0)),
        out_shape=jax.ShapeDtypeStruct((b, t, d), F32),
        scratch_shapes=[pltpu.VMEM((HG_HEADS, HG_DK, HG_DK), F32),
                        pltpu.VMEM((c, d), BF16)],
        compiler_params=_cparams("parallel", "arbitrary"),
        name="hgrn_gla",
    )(proj, h, lb_logits, g_norm.reshape(1, HG_DK), w_out_bf16,
      jnp.asarray(dst, BF16), jnp.asarray(msk, F32))


def _log_sigmoid(x):
    return jnp.minimum(x, 0.0) - jnp.log1p(jnp.exp(-jnp.abs(x)))


def _fox_cum_kernel(fl_ref, bf_ref, utri_ref, ones_ref, o_ref, carry_ref, *, tc):
    @pl.when(pl.program_id(1) == 0)
    def _():
        carry_ref[...] = jnp.zeros_like(carry_ref)

    lf_t = _log_sigmoid(fl_ref[0] + bf_ref[...]).T
    carry = carry_ref[...]
    o_ref[0] = _dot01_r(lf_t, utri_ref[...]) + jnp.tile(carry, (1, tc // LANES))
    carry_ref[...] = carry + _dot01_r(lf_t, ones_ref[...])


def _fox_cumsum(f_logit, b_f_pad, tc):
    b, t, _ = f_logit.shape
    utri = np.triu(np.ones((tc, tc), np.float32))
    return pl.pallas_call(
        functools.partial(_fox_cum_kernel, tc=tc),
        grid=(b, t // tc),
        in_specs=[pl.BlockSpec((1, tc, LANES), lambda i, j: (i, j, 0)),
                  pl.BlockSpec((1, LANES), lambda i, j: (0, 0)),
                  pl.BlockSpec((tc, tc), lambda i, j: (0, 0)),
                  pl.BlockSpec((tc, LANES), lambda i, j: (0, 0))],
        out_specs=pl.BlockSpec((1, LANES, tc), lambda i, j: (i, 0, j)),
        out_shape=jax.ShapeDtypeStruct((b, LANES, t), F32),
        scratch_shapes=[pltpu.VMEM((LANES, LANES), F32)],
        compiler_params=_cparams("parallel", "arbitrary"),
        name="fox_cumsum",
    )(f_logit, b_f_pad, jnp.asarray(utri, BF16), jnp.ones((tc, LANES), BF16))


def _fox_flash_kernel(q_ref, k_ref, v_ref, c_ref, o_ref, m_ref, l_ref, acc_ref, *, tq, tk):
    qi = pl.program_id(2)
    ki = pl.program_id(3)

    @pl.when(ki == 0)
    def _():
        m_ref[...] = jnp.full_like(m_ref, -1e30)
        l_ref[...] = jnp.zeros_like(l_ref)
        acc_ref[...] = jnp.zeros_like(acc_ref)

    lane = lax.broadcasted_iota(jnp.int32, (1, LANES), 1)

    def body(masked):
        q = q_ref[0] * (FOX_DH ** -0.5)
        k = k_ref[0]
        v = v_ref[0]
        if masked:
            row = lax.broadcasted_iota(jnp.int32, (tq, tk), 0)
            col = lax.broadcasted_iota(jnp.int32, (tq, tk), 1)
            causal = row >= col
        for hh in range(2):
            qh = jnp.where((lane < FOX_DH) == (hh == 0), q, jnp.zeros_like(q))
            s = lax.dot_general(qh, k, _NT, preferred_element_type=F32)
            s = s - c_ref[0, 0, hh:hh + 1, :]
            if masked:
                s = jnp.where(causal, s, -1e30)
            m_prev = m_ref[hh]
            m_new = jnp.maximum(m_prev, jnp.max(s, axis=-1, keepdims=True))
            alpha = jnp.exp(m_prev - m_new)
            p = jnp.exp(s - m_new)
            l_ref[hh] = alpha * l_ref[hh] + jnp.sum(p, axis=-1, keepdims=True)
            acc_ref[hh] = alpha * acc_ref[hh] + jnp.dot(p.astype(BF16), v,
                                                        preferred_element_type=F32)
            m_ref[hh] = m_new

    @pl.when(ki < qi)
    def _():
        body(False)

    @pl.when(ki == qi)
    def _():
        body(True)
        o0 = acc_ref[0] / l_ref[0]
        o1 = acc_ref[1] / l_ref[1]
        o_ref[0] = jnp.where(lane < FOX_DH, o0, o1).astype(o_ref.dtype)


def _fox_flash(qkv, cpad, tq):
    b, t, _ = qkv.shape
    npair = FOX_HEADS // 2
    tk = tq
    kv_idx = lambda i, hp, qi, ki: jnp.minimum(ki, qi)
    return pl.pallas_call(
        functools.partial(_fox_flash_kernel, tq=tq, tk=tk),
        grid=(b, npair, t // tq, t // tk),
        in_specs=[pl.BlockSpec((1, tq, LANES), lambda i, hp, qi, ki: (i, qi, hp)),
                  pl.BlockSpec((1, tk, LANES), lambda i, hp, qi, ki: (i, kv_idx(i, hp, qi, ki), npair + hp)),
                  pl.BlockSpec((1, tk, LANES), lambda i, hp, qi, ki: (i, kv_idx(i, hp, qi, ki), 2 * npair + hp)),
                  pl.BlockSpec((1, 1, 8, tk), lambda i, hp, qi, ki: (i, hp, 0, kv_idx(i, hp, qi, ki)))],
        out_specs=pl.BlockSpec((1, tq, LANES), lambda i, hp, qi, ki: (i, qi, hp)),
        out_shape=jax.ShapeDtypeStruct((b, t, D_MODEL), BF16),
        scratch_shapes=[pltpu.VMEM((2, tq, 1), F32),
                        pltpu.VMEM((2, tq, 1), F32),
                        pltpu.VMEM((2, tq, LANES), F32)],
        compiler_params=_cparams("parallel", "parallel", "parallel", "arbitrary"),
        name="fox_flash",
    )(qkv, qkv, qkv, cpad)


def _fox_layer(h, g_mix, w_in, b_f, w_out):
    b, t, d = h.shape
    n = b * t
    h2 = h.reshape(n, d)
    w_qkv = w_in[:, :3 * d].astype(BF16)
    w_f = jnp.pad(w_in[:, 3 * d:], ((0, 0), (0, LANES - FOX_HEADS))).astype(BF16)
    qkv = _rms_proj(h2, g_mix, w_qkv, BF16, 512, 1024).reshape(b, t, 3 * d)
    fl = _rms_proj(h2, g_mix, w_f, F32, 512, LANES).reshape(b, t, LANES)
    b_f_pad = jnp.pad(b_f, (0, LANES - FOX_HEADS)).reshape(1, LANES)
    c_t = _fox_cumsum(fl, b_f_pad, min(512, t))
    cpad = jnp.pad(c_t[:, :FOX_HEADS].reshape(b, FOX_HEADS // 2, 2, t),
                   ((0, 0), (0, 0), (0, 6), (0, 0)))
    o = _fox_flash(qkv, cpad, min(512, t))
    return _mm_res(o.reshape(n, d), w_out.astype(BF16), h2, 512).reshape(b, t, d)


def _softplus(x):
    return jnp.maximum(x, 0.0) + jnp.log1p(jnp.exp(-jnp.abs(x)))


def _ssd_kernel(p_ref, h_ref, cw_ref, cb_ref, dtb_ref, alog_ref, dex_ref, gn_ref, wout_ref,
                ltri_ref, eexp_ref, o_ref, buf_ref, st_ref, y_ref, *, c):
    di = SSM_DI
    gn_w = SSM_GROUPS * SSM_STATE
    gw = di // SSM_GROUPS

    @pl.when(pl.program_id(1) == 0)
    def _():
        buf_ref[0:8, :] = jnp.zeros((8, buf_ref.shape[1]), F32)
        st_ref[...] = jnp.zeros_like(st_ref)

    buf_ref[8:8 + c, :] = p_ref[0, :, di:di + di + 2 * gn_w]
    conv = cb_ref[...]
    for kk in range(SSM_CONV):
        conv = conv + cw_ref[kk:kk + 1, :] * buf_ref[5 + kk:5 + kk + c, :]
    buf_ref[0:8, :] = buf_ref[c:c + 8, :]
    xbc = _silu(conv)
    xs = xbc[:, :di]
    bm = xbc[:, di:di + gn_w]
    cm = xbc[:, di + gn_w:]

    dt = _softplus(p_ref[0, :, 2 * di + 2 * gn_w:] + dtb_ref[...])
    da = dt * (-jnp.exp(alog_ref[...]))
    cum = _dot01(ltri_ref[...], da)
    cum_t = cum.T
    ex = _dot01_r(jnp.concatenate([cum, dt], axis=0), eexp_ref[...])
    cum_e = ex[:c]
    dt_e = ex[c:]
    last = cum_e[c - 1:c]
    expcum = jnp.exp(cum_e)
    declast = jnp.exp(last)
    xdt = xs * dt_e
    xdt_b = xdt.astype(BF16)
    xdec_b = (xdt * jnp.exp(last - cum_e)).astype(BF16)

    row = lax.broadcasted_iota(jnp.int32, (c, c), 0)
    col = lax.broadcasted_iota(jnp.int32, (c, c), 1)
    tril = row >= col
    lane = lax.broadcasted_iota(jnp.int32, (1, LANES), 1)
    for g in range(SSM_GROUPS):
        cg = cm[:, g * SSM_STATE:(g + 1) * SSM_STATE].astype(BF16)
        bg32 = bm[:, g * SSM_STATE:(g + 1) * SSM_STATE]
        cb = lax.dot_general(cg, bg32.astype(BF16), _NT, preferred_element_type=F32)
        st = st_ref[g]
        y_inter = (jnp.dot(cg, st.astype(BF16), preferred_element_type=F32)
                   * expcum[:, g * gw:(g + 1) * gw])
        for pr in range(2):
            base = g * gw + pr * LANES
            xpair = xdt_b[:, base:base + LANES]
            outs = []
            for hh in range(2):
                hidx = g * 4 + pr * 2 + hh
                diff = cum[:, hidx:hidx + 1] - cum_t[hidx:hidx + 1, :]
                lmat = jnp.exp(jnp.where(tril, diff, -1e30))
                outs.append(jnp.dot((cb * lmat).astype(BF16), xpair, preferred_element_type=F32))
            y_intra = jnp.where(lane < SSM_HEADDIM, outs[0], outs[1])
            y_ref[:, base:base + LANES] = (y_intra + y_inter[:, pr * LANES:(pr + 1) * LANES]
                                           + xs[:, base:base + LANES] * dex_ref[:, base:base + LANES])
        st_ref[g] = st * declast[:, g * gw:(g + 1) * gw] + jnp.dot(
            bg32.T.astype(BF16), xdec_b[:, g * gw:(g + 1) * gw], preferred_element_type=F32)

    y = y_ref[...] * _silu(p_ref[0, :, 0:di])
    for g in range(SSM_GROUPS):
        sl = slice(g * gw, (g + 1) * gw)
        y_ref[:, sl] = _rms(y[:, sl], gn_ref[:, sl])
    o_ref[0] = h_ref[0] + jnp.dot(y_ref[...].astype(BF16), wout_ref[...], preferred_element_type=F32)


def _ssd_layer(h, g_mix, w_in, conv_w, conv_b, dt_bias, a_log, d_skip, g_norm, w_out):
    b, t, d = h.shape
    n = b * t
    c = min(SSM_CHUNK, t)
    di = SSM_DI
    cdim = di + 2 * SSM_GROUPS * SSM_STATE
    pad_h = LANES - SSM_HEADS
    w_all = jnp.pad(w_in, ((0, 0), (0, pad_h))).astype(BF16)
    fdim = w_all.shape[1]
    proj = _rms_proj(h.reshape(n, d), g_mix, w_all, F32, 512, 896).reshape(b, t, fdim)
    eexp = np.zeros((LANES, di), np.float32)
    for hd in range(SSM_HEADS):
        eexp[hd, hd * SSM_HEADDIM:(hd + 1) * SSM_HEADDIM] = 1.0
    ltri = np.tril(np.ones((c, c), np.float32))
    full = lambda shape: pl.BlockSpec(shape, lambda i, j: tuple(0 for _ in shape))
    return pl.pallas_call(
        functools.partial(_ssd_kernel, c=c),
        grid=(b, t // c),
        in_specs=[pl.BlockSpec((1, c, fdim), lambda i, j: (i, j, 0)),
                  pl.BlockSpec((1, c, d), lambda i, j: (i, j, 0)),
                  full((SSM_CONV, cdim)), full((1, cdim)), full((1, LANES)), full((1, LANES)),
                  full((1, di)), full((1, di)), full((di, d)), full((c, c)), full((LANES, di))],
        out_specs=pl.BlockSpec((1, c, d), lambda i, j: (i, j, 0)),
        out_shape=jax.ShapeDtypeStruct((b, t, d), F32),
        scratch_shapes=[pltpu.VMEM((c + 8, cdim), F32),
                        pltpu.VMEM((SSM_GROUPS, SSM_STATE, di // SSM_GROUPS), F32),
                        pltpu.VMEM((c, di), F32)],
        compiler_params=_cparams("parallel", "arbitrary"),
        name="ssd_scan",
    )(proj, h, conv_w, conv_b.reshape(1, cdim),
      jnp.pad(dt_bias, (0, pad_h)).reshape(1, LANES), jnp.pad(a_log, (0, pad_h)).reshape(1, LANES),
      jnp.repeat(d_skip, SSM_HEADDIM).reshape(1, di), g_norm.reshape(1, di), w_out.astype(BF16),
      jnp.asarray(ltri, BF16), jnp.asarray(eexp, BF16))


def _top16(s, nrows):
    iota = lax.broadcasted_iota(jnp.int32, (nrows, LANES), 0)
    rank = jnp.full((nrows, LANES), PEER_TOPK, jnp.int32)
    vals = []
    for it in range(PEER_TOPK):
        m = jnp.max(s, axis=0, keepdims=True)
        idx = jnp.min(jnp.where(s == m, iota, nrows), axis=0, keepdims=True)
        sel = iota == idx
        rank = jnp.where(sel, it, rank)
        s = jnp.where(sel, -jnp.inf, s)
        vals.append(m)
    return vals, rank


def _row_pair_words(x, psel):
    y = lax.bitcast_convert_type(jnp.dot(psel, x, preferred_element_type=F32), jnp.uint32)
    half = x.shape[0] // 2
    return (y[:half] >> 16) | y[half:]


def _peer_route_kernel(h_ref, g_ref, wq_ref, keys_ref, psel_ref, xn_ref, r2_ref, e2_ref, na_ref, ca_ref,
                       xs_ref, q_ref, *, mt):
    @pl.when(pl.program_id(1) == 0)
    def _():
        xn = _rms(h_ref[...], g_ref[...]).astype(BF16)
        xs_ref[...] = xn
        xn_ref[...] = xn

    q_ref[...] = jnp.dot(xs_ref[...], wq_ref[...], preferred_element_type=F32).astype(BF16)
    k1 = keys_ref[0, 0]
    k2 = keys_ref[0, 1]

    def sub_tile(sb, carry):
        qs = q_ref[pl.ds(pl.multiple_of(sb * LANES, LANES), LANES), :]
        s1 = lax.dot_general(k1, qs[:, :PEER_HALF], _NT, preferred_element_type=F32)
        s2 = lax.dot_general(k2, qs[:, PEER_HALF:], _NT, preferred_element_type=F32)
        v1, rank1 = _top16(s1, PEER_NKEYS)
        v2, rank2 = _top16(s2, PEER_NKEYS)
        iota16 = lax.broadcasted_iota(jnp.int32, (PEER_TOPK, LANES), 0)
        sv1 = jnp.zeros((PEER_TOPK, LANES), F32)
        for it in range(PEER_TOPK):
            sv1 = jnp.where(iota16 == it, v1[it], sv1)
        cnt = jnp.zeros((PEER_TOPK, LANES), F32)
        front = sv1 + v2[0]
        for _ in range(PEER_TOPK):
            m = jnp.max(front, axis=0, keepdims=True)
            idx = jnp.min(jnp.where(front == m, iota16, PEER_TOPK), axis=0, keepdims=True)
            sel = iota16 == idx
            cnt = jnp.where(sel, cnt + 1.0, cnt)
            n_sel = jnp.sum(jnp.where(sel, cnt, 0.0), axis=0, keepdims=True)
            nxt = jnp.full((1, LANES), -jnp.inf, F32)
            for j in range(1, PEER_TOPK):
                nxt = jnp.where(n_sel == float(j), v2[j], nxt)
            front = jnp.where(sel, sv1 + nxt, front)
        pref = jnp.zeros((1, LANES), F32)
        pn = jnp.zeros((PEER_TOPK, LANES), F32)
        for j in range(PEER_TOPK):
            pref = pref + jnp.exp(v2[j] - v2[0])
            pn = jnp.where(cnt == float(j + 1), pref, pn)
        e1 = jnp.exp(sv1 - v1[0])
        z = jnp.sum(e1 * pn, axis=0, keepdims=True)
        cw = e1 * (1.0 / z)
        na = jnp.zeros((PEER_NKEYS, LANES), F32)
        ca = jnp.zeros((PEER_NKEYS, LANES), F32)
        for i in range(PEER_TOPK):
            hit = rank1 == i
            na = jnp.where(hit, cnt[i:i + 1], na)
            ca = jnp.where(hit, cw[i:i + 1], ca)
        r2_ref[sb, 0] = _row_pair_words(rank2.astype(F32).astype(BF16), psel_ref[...])
        e2_ref[sb, 0] = _row_pair_words(jnp.exp(s2 - v2[0]).astype(BF16), psel_ref[...])
        na_ref[sb, 0] = _pair_words(na)
        ca_ref[sb, 0] = _pair_words(ca)
        return carry

    lax.fori_loop(0, mt // LANES, sub_tile, 0)


def _pair_words(x):
    u = lax.bitcast_convert_type(x.astype(BF16).astype(F32), jnp.uint32)
    return u | (u >> 16)


def _packed_rows(words):
    one = pltpu.bitcast(jnp.broadcast_to(words, (BF16_SUBLANES // 2, LANES)), BF16)
    return jnp.concatenate([one] * (LANES // BF16_SUBLANES), axis=0)


def _gelu(x):
    return 0.5 * x * (1.0 + lax.erf(x * (0.5 ** 0.5)))


def _peer_dense_kernel(xn_ref, u_ref, vt_ref, r2_ref, e2_ref, na_ref, ca_ref, h_ref, o_ref,
                       acc_ref, hida_ref, hidb_ref, xu_ref, *, mt, eb):
    e = pl.program_id(1)
    n_eblk = pl.num_programs(1) - 1

    @pl.when(e == 0)
    def _():
        acc_ref[...] = jnp.zeros_like(acc_ref)
        hidb_ref[...] = jnp.zeros_like(hidb_ref)

    a0 = jnp.minimum(e, n_eblk - 1) * (eb // LANES)
    zero = jnp.zeros((LANES, LANES), BF16)
    nchunk = eb // PEER_CHUNK
    dch = acc_ref.shape[0] // nchunk

    def step(new_ref, old_ref):
        def v_matmul(j):
            acc_ref[j * dch:(j + 1) * dch, :] += jnp.dot(vt_ref[j * dch:(j + 1) * dch, :], old_ref[...],
                                                         preferred_element_type=F32)

        def u_matmul(j):
            rows = slice(j * PEER_CHUNK, (j + 1) * PEER_CHUNK)
            xu_ref[j % 2] = lax.dot_general(u_ref[rows, :], xn_ref[...], _NT,
                                            preferred_element_type=F32)

        def gate_tiles(j, sb):
            n_ablk = PEER_CHUNK // LANES
            gs = [zero] * n_ablk
            for hd in range(PEER_HEADS):
                r2 = pltpu.bitcast(r2_ref[sb, hd], BF16)
                e2 = pltpu.bitcast(e2_ref[sb, hd], BF16)
                for aa in range(n_ablk):
                    a = a0 + j * n_ablk + aa
                    n_row = _packed_rows(na_ref[sb, hd, pl.ds(a, 1), :])
                    c_row = _packed_rows(ca_ref[sb, hd, pl.ds(a, 1), :])
                    gs[aa] = gs[aa] + jnp.where(r2 < n_row, e2, zero) * c_row
            for aa in range(n_ablk):
                r0 = j * PEER_CHUNK + aa * LANES
                act = _gelu(xu_ref[j % 2, aa * LANES:(aa + 1) * LANES, sb * LANES:(sb + 1) * LANES])
                new_ref[r0:r0 + LANES, sb * LANES:(sb + 1) * LANES] = act.astype(BF16) * gs[aa]

        nsub = mt // LANES
        v_matmul(0)
        u_matmul(0)
        for j in range(nchunk):
            if j + 1 < nchunk:
                v_matmul(j + 1)
            for sb in range(nsub // 2):
                gate_tiles(j, sb)
            if j + 1 < nchunk:
                u_matmul(j + 1)
            for sb in range(nsub // 2, nsub):
                gate_tiles(j, sb)

    @pl.when(e % 2 == 0)
    def _():
        step(hida_ref, hidb_ref)

    @pl.when(e % 2 == 1)
    def _():
        step(hidb_ref, hida_ref)

    @pl.when(e == n_eblk)
    def _():
        o_ref[...] = h_ref[...] + acc_ref[...].T


def _peer_layer(h2, g_ffn, w_q, keys, u_tab, v_tab, mt, eb):
    n, d = h2.shape
    n_eblk = u_tab.shape[0] // eb
    nsub = mt // LANES
    half = PEER_NKEYS // 2
    rshape_w = jax.ShapeDtypeStruct((n // LANES, PEER_HEADS, PEER_NKEYS, LANES), jnp.uint32)
    rshape_p = jax.ShapeDtypeStruct((n // LANES, PEER_HEADS, half, LANES), jnp.uint32)
    rspec = pl.BlockSpec((nsub, 1, PEER_NKEYS, LANES), lambda i, hd: (i, hd, 0, 0))
    pspec = pl.BlockSpec((nsub, 1, half, LANES), lambda i, hd: (i, hd, 0, 0))
    psel = np.zeros((PEER_NKEYS, PEER_NKEYS), np.float32)
    psel[np.arange(half), 2 * np.arange(half)] = 1.0
    psel[half + np.arange(half), 2 * np.arange(half) + 1] = 1.0
    xn, r2, e2, na, ca = pl.pallas_call(
        functools.partial(_peer_route_kernel, mt=mt),
        grid=(n // mt, PEER_HEADS),
        in_specs=[pl.BlockSpec((mt, d), lambda i, hd: (i, 0)),
                  pl.BlockSpec((1, d), lambda i, hd: (0, 0)),
                  pl.BlockSpec((d, 2 * PEER_HALF), lambda i, hd: (0, hd)),
                  pl.BlockSpec((1, 2, PEER_NKEYS, PEER_HALF), lambda i, hd: (hd, 0, 0, 0)),
                  pl.BlockSpec((PEER_NKEYS, PEER_NKEYS), lambda i, hd: (0, 0))],
        out_specs=[pl.BlockSpec((mt, d), lambda i, hd: (i, 0)), pspec, pspec, rspec, rspec],
        out_shape=[jax.ShapeDtypeStruct((n, d), BF16), rshape_p, rshape_p, rshape_w, rshape_w],
        scratch_shapes=[pltpu.VMEM((mt, d), BF16), pltpu.VMEM((mt, 2 * PEER_HALF), BF16)],
        compiler_params=_cparams("parallel", "arbitrary"),
        name="peer_route",
    )(h2, g_ffn.reshape(1, d), w_q.astype(BF16), keys.astype(BF16), jnp.asarray(psel, BF16))

    dspec = pl.BlockSpec((nsub, PEER_HEADS, PEER_NKEYS, LANES), lambda i, e: (i, 0, 0, 0))
    dspec_p = pl.BlockSpec((nsub, PEER_HEADS, half, LANES), lambda i, e: (i, 0, 0, 0))
    return pl.pallas_call(
        functools.partial(_peer_dense_kernel, mt=mt, eb=eb),
        grid=(n // mt, n_eblk + 1),
        in_specs=[pl.BlockSpec((mt, d), lambda i, e: (i, 0)),
                  pl.BlockSpec((eb, d), lambda i, e: (jnp.minimum(e, n_eblk - 1), 0)),
                  pl.BlockSpec((d, eb), lambda i, e: (0, jnp.maximum(e - 1, 0))),
                  dspec_p, dspec_p, dspec, dspec,
                  pl.BlockSpec((mt, d), lambda i, e: (i, 0))],
        out_specs=pl.BlockSpec((mt, d), lambda i, e: (i, 0)),
        out_shape=jax.ShapeDtypeStruct((n, d), F32),
        scratch_shapes=[pltpu.VMEM((d, mt), F32), pltpu.VMEM((eb, mt), BF16),
                        pltpu.VMEM((eb, mt), BF16), pltpu.VMEM((2, PEER_CHUNK, mt), F32)],
        compiler_params=_cparams("parallel", "arbitrary"),
        name="peer_dense",
    )(xn, u_tab.astype(BF16), v_tab.T.astype(BF16), r2, e2, na, ca, h2)


def kernel(x, norm_mix, norm_ffn, norm_final, hgrn_lb_logits, hgrn_w_in, hgrn_gnorm, hgrn_w_out,
           fox_w_in, fox_b_f, fox_w_out, ssm_w_in, ssm_conv_w, ssm_conv_b, ssm_dt_bias, ssm_a_log,
           ssm_d, ssm_gnorm, ssm_w_out, peer_w_q, peer_keys, peer_u, peer_v):
    b, t, d = x.shape
    n = b * t
    depth = norm_mix.shape[0]
    h = x
    for i in range(depth):
        kind, j = i % 3, i // 3
        if kind == 0:
            proj = _rms_proj(h.reshape(n, d), norm_mix[i], hgrn_w_in[j].astype(BF16), F32, 512, 1024)
            h = _hgrn_layer(h, proj.reshape(b, t, 4 * d), hgrn_lb_logits, hgrn_gnorm[j],
                            hgrn_w_out[j].astype(BF16), i)
        elif kind == 1:
            h = _fox_layer(h, norm_mix[i], fox_w_in[j], fox_b_f[j], fox_w_out[j])
        else:
            h = _ssd_layer(h, norm_mix[i], ssm_w_in[j], ssm_conv_w[j], ssm_conv_b[j], ssm_dt_bias[j],
                           ssm_a_log[j], ssm_d[j], ssm_gnorm[j], ssm_w_out[j])
        h = _peer_layer(h.reshape(n, d), norm_ffn[i], peer_w_q[i], peer_keys[i], peer_u[i],
                        peer_v[i], min(512, n), 1024).reshape(b, t, d)
    return _final_norm(h.reshape(n, d), norm_final, 512).reshape(b, t, d)
```

```python
import functools
import math

import numpy as np
import jax
import jax.numpy as jnp
from jax import lax
from jax.experimental import pallas as pl
from jax.experimental.pallas import tpu as pltpu

F32 = jnp.float32
BF16 = jnp.bfloat16

NORM_EPS = 1e-6
LANES = 128
BF16_SUBLANES = 16
VMEM_LIMIT = 56 * 1024 * 1024

D_MODEL = 1024
HG_HEADS = 8
HG_DK = 128
GLA_CHUNK = 128
GLA_SUB = 16
FOX_HEADS = 16
FOX_DH = 64
FOX_TILE = 1024
SSM_DI = 2048
SSM_HEADS = 32
SSM_HEADDIM = 64
SSM_GROUPS = 8
SSM_STATE = 128
SSM_CONV = 4
SSM_CHUNK = 128
PEER_HEADS = 8
PEER_NKEYS = 128
PEER_TOPK = 16
PEER_HALF = 128
PEER_CHUNK = 256

_NT = (((1,), (1,)), ((), ()))
_TN = (((0,), (0,)), ((), ()))


def _cparams(*sem):
    return pltpu.CompilerParams(dimension_semantics=sem, vmem_limit_bytes=VMEM_LIMIT)


def _split3(x):
    hi = x.astype(BF16)
    r = x - hi.astype(F32)
    mid = r.astype(BF16)
    lo = (r - mid.astype(F32)).astype(BF16)
    return hi, mid, lo


def _dot01(m01, x):
    hi, mid, lo = _split3(x)
    d = lambda b: jnp.dot(m01, b, preferred_element_type=F32)
    return d(hi) + d(mid) + d(lo)


def _dot01_r(x, m01):
    hi, mid, lo = _split3(x)
    d = lambda a: jnp.dot(a, m01, preferred_element_type=F32)
    return d(hi) + d(mid) + d(lo)


def _sigmoid(x):
    return 1.0 / (1.0 + jnp.exp(-x))


def _silu(x):
    return x * _sigmoid(x)


def _rms(x, g):
    ms = jnp.mean(x * x, axis=-1, keepdims=True)
    return x * lax.rsqrt(ms + NORM_EPS) * g


def _rms_proj_kernel(x_ref, g_ref, w_ref, o_ref, xn_ref):
    @pl.when(pl.program_id(1) == 0)
    def _():
        xn_ref[...] = _rms(x_ref[...], g_ref[...]).astype(BF16)

    o_ref[...] = jnp.dot(xn_ref[...], w_ref[...], preferred_element_type=F32).astype(o_ref.dtype)


def _rms_proj(x2d, g, w_bf16, out_dtype, tm, tn):
    n, d = x2d.shape
    f = w_bf16.shape[1]
    assert n % tm == 0 and f % tn == 0
    return pl.pallas_call(
        _rms_proj_kernel,
        grid=(n // tm, f // tn),
        in_specs=[pl.BlockSpec((tm, d), lambda i, j: (i, 0)),
                  pl.BlockSpec((1, d), lambda i, j: (0, 0)),
                  pl.BlockSpec((d, tn), lambda i, j: (0, j))],
        out_specs=pl.BlockSpec((tm, tn), lambda i, j: (i, j)),
        out_shape=jax.ShapeDtypeStruct((n, f), out_dtype),
        scratch_shapes=[pltpu.VMEM((tm, d), BF16)],
        compiler_params=_cparams("parallel", "arbitrary"),
        name="rms_proj",
    )(x2d, g.reshape(1, d), w_bf16)


def _mm_res_kernel(x_ref, w_ref, r_ref, o_ref):
    o_ref[...] = r_ref[...] + jnp.dot(x_ref[...], w_ref[...], preferred_element_type=F32)


def _mm_res(x2d_bf16, w_bf16, res2d, tm):
    n, k = x2d_bf16.shape
    d = w_bf16.shape[1]
    return pl.pallas_call(
        _mm_res_kernel,
        grid=(n // tm,),
        in_specs=[pl.BlockSpec((tm, k), lambda i: (i, 0)),
                  pl.BlockSpec((k, d), lambda i: (0, 0)),
                  pl.BlockSpec((tm, d), lambda i: (i, 0))],
        out_specs=pl.BlockSpec((tm, d), lambda i: (i, 0)),
        out_shape=jax.ShapeDtypeStruct((n, d), F32),
        compiler_params=_cparams("parallel"),
        name="mm_res",
    )(x2d_bf16, w_bf16, res2d)


def _final_norm_kernel(x_ref, g_ref, o_ref):
    o_ref[...] = _rms(x_ref[...], g_ref[...])


def _final_norm(x2d, g, tm):
    n, d = x2d.shape
    return pl.pallas_call(
        _final_norm_kernel,
        grid=(n // tm,),
        in_specs=[pl.BlockSpec((tm, d), lambda i: (i, 0)),
                  pl.BlockSpec((1, d), lambda i: (0, 0))],
        out_specs=pl.BlockSpec((tm, d), lambda i: (i, 0)),
        out_shape=jax.ShapeDtypeStruct((n, d), F32),
        compiler_params=_cparams("parallel"),
        name="final_norm",
    )(x2d, g.reshape(1, d))


def _gla_tables(c, sub):
    t = np.arange(c)[:, None]
    j = np.arange(c)[None, :]
    mats = [(j <= t).astype(np.float32),
            (j > t).astype(np.float32)]
    masks = []
    bk = c // 2
    while bk >= sub:
        blk = t // bk
        odd = (blk % 2) == 1
        mats.append((odd & (j >= blk * bk) & (j <= t)).astype(np.float32))
        mats.append(((~odd) & (j > t) & (j < (blk + 1) * bk)).astype(np.float32))
        masks.append((odd & ((j // bk) == blk - 1)).astype(np.float32))
        bk //= 2
    m0 = (t // sub) * sub
    dq = ((j >= m0) & (j <= t)).astype(np.float32)
    mats += [dq, -dq]
    masks.append((((t // sub) == (j // sub)) & (j <= t)).astype(np.float32))
    return np.concatenate(mats, axis=0), np.stack(masks, axis=0)


def _hgrn_kernel(proj_ref, h_ref, lbl_ref, gn_ref, wout_ref, dst_ref, msk_ref, o_ref,
                 st_ref, y_ref, *, layer, depth, c, nlev):
    @pl.when(pl.program_id(1) == 0)
    def _():
        st_ref[...] = jnp.zeros_like(st_ref)

    rows = [lbl_ref[j:j + 1, :] for j in range(depth)]
    mx = functools.reduce(jnp.maximum, rows)
    es = [jnp.exp(r - mx) for r in rows]
    tot = functools.reduce(lambda a, b: a + b, es)
    lb = jnp.zeros_like(mx)
    for j in range(1, layer + 1):
        lb = lb + es[j] / tot

    d = HG_HEADS * HG_DK
    q = proj_ref[0, :, 0:d] * (HG_DK ** -0.5)
    f = lb + (1.0 - lb) * _sigmoid(proj_ref[0, :, d:2 * d])
    k = 1.0 - f
    ex = jnp.exp(_dot01(dst_ref[...], jnp.log(f)))
    q_dec = (q * ex[0:c]).astype(BF16)
    k_dec = (k * ex[c:2 * c]).astype(BF16)
    dec_last = ex[c - 1:c]
    qs, ks = [], []
    for l in range(nlev):
        qs.append((q * ex[(2 + 2 * l) * c:(3 + 2 * l) * c]).astype(BF16))
        ks.append((k * ex[(3 + 2 * l) * c:(4 + 2 * l) * c]).astype(BF16))
    v = proj_ref[0, :, 2 * d:3 * d].astype(BF16)
    gate = proj_ref[0, :, 3 * d:4 * d]
    gn = gn_ref[...]
    for hd in range(HG_HEADS):
        sl = slice(hd * HG_DK, (hd + 1) * HG_DK)
        a = jnp.zeros((c, c), F32)
        for l in range(nlev):
            a = a + lax.dot_general(qs[l][:, sl], ks[l][:, sl], _NT,
                                    preferred_element_type=F32) * msk_ref[l]
        st = st_ref[hd]
        o = (jnp.dot(a.astype(BF16), v[:, sl], preferred_element_type=F32)
             + lax.dot_general(q_dec[:, sl], st.astype(BF16), _NT, preferred_element_type=F32))
        st_ref[hd] = st * dec_last[:, sl] + lax.dot_general(
            v[:, sl], k_dec[:, sl], _TN, preferred_element_type=F32)
        y_ref[:, sl] = (_rms(o, gn) * _silu(gate[:, sl])).astype(BF16)
    o_ref[0] = h_ref[0] + jnp.dot(y_ref[...], wout_ref[...], preferred_element_type=F32)


def _hgrn_layer(h, proj, lb_logits, g_norm, w_out_bf16, layer):
    b, t, d = h.shape
    c = GLA_CHUNK
    dst, msk = _gla_tables(c, GLA_SUB)
    nlev = msk.shape[0]
    depth = lb_logits.shape[0]
    kern = functools.partial(_hgrn_kernel, layer=layer, depth=depth, c=c, nlev=nlev)
    return pl.pallas_call(
        kern,
        grid=(b, t // c),
        in_specs=[pl.BlockSpec((1, c, 4 * d), lambda i, j: (i, j, 0)),
                  pl.BlockSpec((1, c, d), lambda i, j: (i, j, 0)),
                  pl.BlockSpec((depth, d), lambda i, j: (0, 0)),
                  pl.BlockSpec((1, HG_DK), lambda i, j: (0, 0)),
                  pl.BlockSpec((d, d), lambda i, j: (0, 0)),
                  pl.BlockSpec(dst.shape, lambda i, j: (0, 0)),
                  pl.BlockSpec(msk.shape, lambda i, j: (0, 0, 0))],
        out_specs=pl.BlockSpec((1, c, d), lambda i, j: (i, j, 0)),
        out_shape=jax.ShapeDtypeStruct((b, t, d), F32),
        scratch_shapes=[pltpu.VMEM((HG_HEADS, HG_DK, HG_DK), F32),
                        pltpu.VMEM((c, d), BF16)],
        compiler_params=_cparams("parallel", "arbitrary"),
        name="hgrn_gla",
    )(proj, h, lb_logits, g_norm.reshape(1, HG_DK), w_out_bf16,
      jnp.asarray(dst, BF16), jnp.asarray(msk, F32))


def _log_sigmoid(x):
    return jnp.minimum(x, 0.0) - jnp.log1p(jnp.exp(-jnp.abs(x)))


def _fox_cum_kernel(fl_ref, bf_ref, ltri_ref, place_ref, o_ref, carry_ref, *, tc):
    @pl.when(pl.program_id(1) == 0)
    def _():
        carry_ref[...] = jnp.zeros_like(carry_ref)

    lf = _log_sigmoid(fl_ref[0] + bf_ref[...])
    c = _dot01(ltri_ref[...], lf) + carry_ref[...]
    carry_ref[...] = c[tc - 1:tc, :]
    hi, mid, lo = _split3(-c)
    o_ref[0] = jnp.dot(jnp.concatenate([hi, mid, lo], axis=1), place_ref[...],
                       preferred_element_type=F32).astype(BF16)


def _fox_cumsum(f_logit, b_f_pad, tc):
    b, t, _ = f_logit.shape
    ltri = np.tril(np.ones((tc, tc), np.float32))
    place = np.zeros((3 * LANES, FOX_HEADS * LANES), np.float32)
    for hd in range(FOX_HEADS):
        for part in range(3):
            place[part * LANES + hd, hd * LANES + part] = 1.0
    return pl.pallas_call(
        functools.partial(_fox_cum_kernel, tc=tc),
        grid=(b, t // tc),
        in_specs=[pl.BlockSpec((1, tc, LANES), lambda i, j: (i, j, 0)),
                  pl.BlockSpec((1, LANES), lambda i, j: (0, 0)),
                  pl.BlockSpec((tc, tc), lambda i, j: (0, 0)),
                  pl.BlockSpec(place.shape, lambda i, j: (0, 0))],
        out_specs=pl.BlockSpec((1, tc, FOX_HEADS * LANES), lambda i, j: (i, j, 0)),
        out_shape=jax.ShapeDtypeStruct((b, t, FOX_HEADS * LANES), BF16),
        scratch_shapes=[pltpu.VMEM((1, LANES), F32)],
        compiler_params=_cparams("parallel", "arbitrary"),
        name="fox_cumsum",
    )(f_logit, b_f_pad, jnp.asarray(ltri, BF16), jnp.asarray(place, BF16))


def _fox_flash_kernel(q_ref, k_ref, cf_ref, vt_ref, o_ref, qp_ref, m_ref, l_ref, acc_ref, *, tq, tk):
    qi = pl.program_id(2)
    ki = pl.program_id(3)
    lane = lax.broadcasted_iota(jnp.int32, (1, LANES), 1)

    @pl.when(ki == 0)
    def _():
        m_ref[...] = jnp.full_like(m_ref, -1e30)
        l_ref[...] = jnp.zeros_like(l_ref)
        acc_ref[...] = jnp.zeros_like(acc_ref)
        q = q_ref[0] * (FOX_DH ** -0.5)
        ones3 = jnp.broadcast_to(jnp.where(lane < 3, 1.0, 0.0).astype(BF16), (tq, LANES))
        for hh in range(2):
            qh = jnp.where((lane < FOX_DH) == (hh == 0), q, jnp.zeros_like(q))
            qp_ref[hh] = jnp.concatenate([qh, ones3], axis=1)

    def body(masked):
        k = k_ref[0]
        st = []
        for hh in range(2):
            kp = jnp.concatenate([k, cf_ref[0, :, hh * LANES:(hh + 1) * LANES]], axis=1)
            st.append(lax.dot_general(kp, qp_ref[hh], _NT, preferred_element_type=F32))
        if masked:
            key_pos = lax.broadcasted_iota(jnp.int32, (tk, tq), 0)
            qry_pos = lax.broadcasted_iota(jnp.int32, (tk, tq), 1)
            causal = key_pos <= qry_pos
        for hh in range(2):
            s = jnp.where(causal, st[hh], -1e30) if masked else st[hh]
            m_prev = m_ref[hh]
            m_new = jnp.maximum(m_prev, jnp.max(s, axis=0, keepdims=True))
            alpha = jnp.exp(m_prev - m_new)
            p = jnp.exp(s - m_new)
            l_ref[hh] = alpha * l_ref[hh] + jnp.sum(p, axis=0, keepdims=True)
            acc_ref[hh] = alpha * acc_ref[hh] + jnp.dot(
                vt_ref[0, hh * FOX_DH:(hh + 1) * FOX_DH, :], p.astype(BF16), preferred_element_type=F32)
            m_ref[hh] = m_new

    @pl.when(ki < qi)
    def _():
        body(False)

    @pl.when(ki == qi)
    def _():
        body(True)
        o_ref[0] = jnp.concatenate([acc_ref[0] / l_ref[0], acc_ref[1] / l_ref[1]], axis=0).astype(o_ref.dtype)


def _fox_flash(qk, cfeat, v_t, tq):
    b, t, _ = qk.shape
    npair = FOX_HEADS // 2
    tk = tq
    kv = lambda qi, ki: jnp.minimum(ki, qi)
    return pl.pallas_call(
        functools.partial(_fox_flash_kernel, tq=tq, tk=tk),
        grid=(b, npair, t // tq, t // tk),
        in_specs=[pl.BlockSpec((1, tq, LANES), lambda i, hp, qi, ki: (i, qi, hp)),
                  pl.BlockSpec((1, tk, LANES), lambda i, hp, qi, ki: (i, kv(qi, ki), npair + hp)),
                  pl.BlockSpec((1, tk, 2 * LANES), lambda i, hp, qi, ki: (i, kv(qi, ki), hp)),
                  pl.BlockSpec((1, LANES, tk), lambda i, hp, qi, ki: (i, hp, kv(qi, ki)))],
        out_specs=pl.BlockSpec((1, LANES, tq), lambda i, hp, qi, ki: (i, hp, qi)),
        out_shape=jax.ShapeDtypeStruct((b, D_MODEL, t), BF16),
        scratch_shapes=[pltpu.VMEM((2, tq, 2 * LANES), BF16),
                        pltpu.VMEM((2, 1, tq), F32),
                        pltpu.VMEM((2, 1, tq), F32),
                        pltpu.VMEM((2, FOX_DH, tq), F32)],
        compiler_params=_cparams("parallel", "parallel", "parallel", "arbitrary"),
        name="fox_flash",
    )(qk, qk, cfeat, v_t)


def _rms_proj_t_kernel(x_ref, g_ref, wt_ref, o_ref):
    xn = _rms(x_ref[0], g_ref[...]).astype(BF16)
    o_ref[0] = lax.dot_general(wt_ref[...], xn, _NT, preferred_element_type=F32).astype(o_ref.dtype)


def _rms_proj_t(x, g, wt_bf16, tm):
    b, t, d = x.shape
    f = wt_bf16.shape[0]
    return pl.pallas_call(
        _rms_proj_t_kernel,
        grid=(b, t // tm),
        in_specs=[pl.BlockSpec((1, tm, d), lambda i, j: (i, j, 0)),
                  pl.BlockSpec((1, d), lambda i, j: (0, 0)),
                  pl.BlockSpec((f, d), lambda i, j: (0, 0))],
        out_specs=pl.BlockSpec((1, f, tm), lambda i, j: (i, 0, j)),
        out_shape=jax.ShapeDtypeStruct((b, f, t), BF16),
        compiler_params=_cparams("parallel", "parallel"),
        name="rms_proj_t",
    )(x, g.reshape(1, d), wt_bf16)


def _mm_t_res_kernel(xt_ref, w_ref, r_ref, o_ref):
    o_ref[0] = r_ref[0] + lax.dot_general(xt_ref[0], w_ref[...], _TN, preferred_element_type=F32)


def _mm_t_res(x_t, w_bf16, res, tm):
    b, k, t = x_t.shape
    d = w_bf16.shape[1]
    return pl.pallas_call(
        _mm_t_res_kernel,
        grid=(b, t // tm),
        in_specs=[pl.BlockSpec((1, k, tm), lambda i, j: (i, 0, j)),
                  pl.BlockSpec((k, d), lambda i, j: (0, 0)),
                  pl.BlockSpec((1, tm, d), lambda i, j: (i, j, 0))],
        out_specs=pl.BlockSpec((1, tm, d), lambda i, j: (i, j, 0)),
        out_shape=jax.ShapeDtypeStruct((b, t, d), F32),
        compiler_params=_cparams("parallel", "parallel"),
        name="mm_t_res",
    )(x_t, w_bf16, res)


def _fox_layer(h, g_mix, w_in, b_f, w_out):
    b, t, d = h.shape
    n = b * t
    h2 = h.reshape(n, d)
    w_qk = w_in[:, :2 * d].astype(BF16)
    w_vt = w_in[:, 2 * d:3 * d].T.astype(BF16)
    w_f = jnp.pad(w_in[:, 3 * d:], ((0, 0), (0, LANES - FOX_HEADS))).astype(BF16)
    qk = _rms_proj(h2, g_mix, w_qk, BF16, 512, 1024).reshape(b, t, 2 * d)
    v_t = _rms_proj_t(h, g_mix, w_vt, min(512, t))
    fl = _rms_proj(h2, g_mix, w_f, F32, 512, LANES).reshape(b, t, LANES)
    b_f_pad = jnp.pad(b_f, (0, LANES - FOX_HEADS)).reshape(1, LANES)
    cfeat = _fox_cumsum(fl, b_f_pad, min(512, t))
    o_t = _fox_flash(qk, cfeat, v_t, min(FOX_TILE, t))
    return _mm_t_res(o_t, w_out.astype(BF16), h, min(512, t))


def _softplus(x):
    return jnp.maximum(x, 0.0) + jnp.log1p(jnp.exp(-jnp.abs(x)))


def _ssd_kernel(p_ref, h_ref, cw_ref, cb_ref, dtb_ref, alog_ref, dex_ref, gn_ref, wout_ref,
                ltri_ref, eexp_ref, o_ref, buf_ref, st_ref, y_ref, *, c):
    di = SSM_DI
    gn_w = SSM_GROUPS * SSM_STATE
    gw = di // SSM_GROUPS

    @pl.when(pl.program_id(1) == 0)
    def _():
        buf_ref[0:8, :] = jnp.zeros((8, buf_ref.shape[1]), F32)
        st_ref[...] = jnp.zeros_like(st_ref)

    buf_ref[8:8 + c, :] = p_ref[0, :, di:di + di + 2 * gn_w]
    conv = cb_ref[...]
    for kk in range(SSM_CONV):
        conv = conv + cw_ref[kk:kk + 1, :] * buf_ref[5 + kk:5 + kk + c, :]
    buf_ref[0:8, :] = buf_ref[c:c + 8, :]
    xbc = _silu(conv)
    xs = xbc[:, :di]
    bm = xbc[:, di:di + gn_w]
    cm = xbc[:, di + gn_w:]

    dt = _softplus(p_ref[0, :, 2 * di + 2 * gn_w:] + dtb_ref[...])
    da = dt * (-jnp.exp(alog_ref[...]))
    cum = _dot01(ltri_ref[...], da)
    cum_t = cum.T
    ex = _dot01_r(jnp.concatenate([cum, dt], axis=0), eexp_ref[...])
    cum_e = ex[:c]
    dt_e = ex[c:]
    last = cum_e[c - 1:c]
    expcum = jnp.exp(cum_e)
    declast = jnp.exp(last)
    xdt = xs * dt_e
    xdt_b = xdt.astype(BF16)
    xdec_b = (xdt * jnp.exp(last - cum_e)).astype(BF16)

    row = lax.broadcasted_iota(jnp.int32, (c, c), 0)
    col = lax.broadcasted_iota(jnp.int32, (c, c), 1)
    tril = row >= col
    lane = lax.broadcasted_iota(jnp.int32, (1, LANES), 1)
    for g in range(SSM_GROUPS):
        cg = cm[:, g * SSM_STATE:(g + 1) * SSM_STATE].astype(BF16)
        bg32 = bm[:, g * SSM_STATE:(g + 1) * SSM_STATE]
        cb = lax.dot_general(cg, bg32.astype(BF16), _NT, preferred_element_type=F32)
        st = st_ref[g]
        y_inter = (jnp.dot(cg, st.astype(BF16), preferred_element_type=F32)
                   * expcum[:, g * gw:(g + 1) * gw])
        for pr in range(2):
            base = g * gw + pr * LANES
            xpair = xdt_b[:, base:base + LANES]
            outs = []
            for hh in range(2):
                hidx = g * 4 + pr * 2 + hh
                diff = cum[:, hidx:hidx + 1] - cum_t[hidx:hidx + 1, :]
                lmat = jnp.exp(jnp.where(tril, diff, -1e30))
                outs.append(jnp.dot((cb * lmat).astype(BF16), xpair, preferred_element_type=F32))
            y_intra = jnp.where(lane < SSM_HEADDIM, outs[0], outs[1])
            y_ref[:, base:base + LANES] = (y_intra + y_inter[:, pr * LANES:(pr + 1) * LANES]
                                           + xs[:, base:base + LANES] * dex_ref[:, base:base + LANES])
        st_ref[g] = st * declast[:, g * gw:(g + 1) * gw] + jnp.dot(
            bg32.T.astype(BF16), xdec_b[:, g * gw:(g + 1) * gw], preferred_element_type=F32)

    y = y_ref[...] * _silu(p_ref[0, :, 0:di])
    for g in range(SSM_GROUPS):
        sl = slice(g * gw, (g + 1) * gw)
        y_ref[:, sl] = _rms(y[:, sl], gn_ref[:, sl])
    o_ref[0] = h_ref[0] + jnp.dot(y_ref[...].astype(BF16), wout_ref[...], preferred_element_type=F32)


def _ssd_layer(h, g_mix, w_in, conv_w, conv_b, dt_bias, a_log, d_skip, g_norm, w_out):
    b, t, d = h.shape
    n = b * t
    c = min(SSM_CHUNK, t)
    di = SSM_DI
    cdim = di + 2 * SSM_GROUPS * SSM_STATE
    pad_h = LANES - SSM_HEADS
    w_all = jnp.pad(w_in, ((0, 0), (0, pad_h))).astype(BF16)
    fdim = w_all.shape[1]
    proj = _rms_proj(h.reshape(n, d), g_mix, w_all, F32, 512, 896).reshape(b, t, fdim)
    eexp = np.zeros((LANES, di), np.float32)
    for hd in range(SSM_HEADS):
        eexp[hd, hd * SSM_HEADDIM:(hd + 1) * SSM_HEADDIM] = 1.0
    ltri = np.tril(np.ones((c, c), np.float32))
    full = lambda shape: pl.BlockSpec(shape, lambda i, j: tuple(0 for _ in shape))
    return pl.pallas_call(
        functools.partial(_ssd_kernel, c=c),
        grid=(b, t // c),
        in_specs=[pl.BlockSpec((1, c, fdim), lambda i, j: (i, j, 0)),
                  pl.BlockSpec((1, c, d), lambda i, j: (i, j, 0)),
                  full((SSM_CONV, cdim)), full((1, cdim)), full((1, LANES)), full((1, LANES)),
                  full((1, di)), full((1, di)), full((di, d)), full((c, c)), full((LANES, di))],
        out_specs=pl.BlockSpec((1, c, d), lambda i, j: (i, j, 0)),
        out_shape=jax.ShapeDtypeStruct((b, t, d), F32),
        scratch_shapes=[pltpu.VMEM((c + 8, cdim), F32),
                        pltpu.VMEM((SSM_GROUPS, SSM_STATE, di // SSM_GROUPS), F32),
                        pltpu.VMEM((c, di), F32)],
        compiler_params=_cparams("parallel", "arbitrary"),
        name="ssd_scan",
    )(proj, h, conv_w, conv_b.reshape(1, cdim),
      jnp.pad(dt_bias, (0, pad_h)).reshape(1, LANES), jnp.pad(a_log, (0, pad_h)).reshape(1, LANES),
      jnp.repeat(d_skip, SSM_HEADDIM).reshape(1, di), g_norm.reshape(1, di), w_out.astype(BF16),
      jnp.asarray(ltri, BF16), jnp.asarray(eexp, BF16))


def _stack_rows(rows):
    iota16 = lax.broadcasted_iota(jnp.int32, (PEER_TOPK, LANES), 0)
    out = jnp.zeros((PEER_TOPK, LANES), F32)
    for it, r in enumerate(rows):
        out = jnp.where(iota16 == it, r, out)
    return out


def _top16(s, exact):
    nrows = s.shape[0]
    iota = lax.broadcasted_iota(jnp.int32, (nrows, LANES), 0).astype(F32)
    rank = jnp.full((nrows, LANES), float(PEER_TOPK), F32)
    vals = []
    for it in range(PEER_TOPK):
        m = jnp.max(s, axis=0, keepdims=True)
        sel = s == m
        if exact:
            sel = iota == jnp.min(jnp.where(sel, iota, float(nrows)), axis=0, keepdims=True)
        rank = jnp.where(sel, float(it), rank)
        s = jnp.where(sel, -jnp.inf, s)
        vals.append(m)
    return _stack_rows(vals), rank


def _top16_all(scores):
    fast = [_top16(s, False) for s in scores]
    picked = jnp.zeros((1, LANES), F32)
    for _, r in fast:
        picked = picked + jnp.sum(jnp.where(r < float(PEER_TOPK), 1.0, 0.0), axis=0, keepdims=True)
    tied = jnp.max(jnp.abs(picked - float(PEER_TOPK * len(scores)))) > 0.0
    flat = lax.cond(tied,
                    lambda: tuple(x for s in scores for x in _top16(s, True)),
                    lambda: tuple(x for vr in fast for x in vr))
    return [(flat[2 * i], flat[2 * i + 1]) for i in range(len(scores))]


def _row_pair_words(x, psel):
    y = lax.bitcast_convert_type(jnp.dot(psel, x, preferred_element_type=F32), jnp.uint32)
    half = x.shape[0] // 2
    return (y[:half] >> 16) | y[half:]


def _peer_route_kernel(h_ref, g_ref, wq_ref, keys_ref, psel_ref, xn_ref, r2_ref, e2_ref, na_ref, ca_ref,
                       xs_ref, q_ref, *, mt):
    @pl.when(pl.program_id(1) == 0)
    def _():
        xn = _rms(h_ref[...], g_ref[...]).astype(BF16)
        xs_ref[...] = xn
        xn_ref[...] = xn

    q_ref[...] = jnp.dot(xs_ref[...], wq_ref[...], preferred_element_type=F32).astype(BF16)
    k1 = keys_ref[0, 0]
    k2 = keys_ref[0, 1]

    nsub = mt // LANES
    s1s, s2s = [], []
    for sb in range(nsub):
        qs = q_ref[sb * LANES:(sb + 1) * LANES, :]
        s1s.append(lax.dot_general(k1, qs[:, :PEER_HALF], _NT, preferred_element_type=F32))
        s2s.append(lax.dot_general(k2, qs[:, PEER_HALF:], _NT, preferred_element_type=F32))
    tops = _top16_all(s1s + s2s)
    iota16 = lax.broadcasted_iota(jnp.int32, (PEER_TOPK, LANES), 0).astype(F32)

    cnts = [jnp.zeros((PEER_TOPK, LANES), F32) for _ in range(nsub)]
    fronts = [tops[sb][0] + tops[nsub + sb][0][0:1] for sb in range(nsub)]
    for _ in range(PEER_TOPK):
        for sb in range(nsub):
            sv1, sv2, front, cnt = tops[sb][0], tops[nsub + sb][0], fronts[sb], cnts[sb]
            m = jnp.max(front, axis=0, keepdims=True)
            idx = jnp.min(jnp.where(front == m, iota16, float(PEER_TOPK)), axis=0, keepdims=True)
            sel = iota16 == idx
            cnt = jnp.where(sel, cnt + 1.0, cnt)
            n_sel = jnp.max(jnp.where(sel, cnt, 0.0), axis=0, keepdims=True)
            nxt = jnp.max(jnp.where(iota16 == n_sel, sv2, -jnp.inf), axis=0, keepdims=True)
            fronts[sb] = jnp.where(sel, sv1 + nxt, front)
            cnts[sb] = cnt

    for sb in range(nsub):
        sv1, rank1 = tops[sb]
        sv2, rank2 = tops[nsub + sb]
        cnt = cnts[sb]
        s2 = s2s[sb]
        v2 = [sv2[j:j + 1] for j in range(PEER_TOPK)]
        e2r = jnp.exp(sv2 - v2[0])
        pref = jnp.zeros((1, LANES), F32)
        pn = jnp.zeros((PEER_TOPK, LANES), F32)
        for j in range(PEER_TOPK):
            pref = pref + e2r[j:j + 1]
            pn = jnp.where(cnt == float(j + 1), pref, pn)
        e1 = jnp.exp(sv1 - sv1[0:1])
        z = jnp.sum(e1 * pn, axis=0, keepdims=True)
        cw = e1 * (1.0 / z)
        na = jnp.zeros((PEER_NKEYS, LANES), F32)
        ca = jnp.zeros((PEER_NKEYS, LANES), F32)
        for i in range(PEER_TOPK):
            hit = rank1 == float(i)
            na = jnp.where(hit, cnt[i:i + 1], na)
            ca = jnp.where(hit, cw[i:i + 1], ca)
        r2_ref[sb, 0] = _row_pair_words(rank2.astype(BF16), psel_ref[...])
        e2_ref[sb, 0] = _row_pair_words(jnp.exp(s2 - v2[0]).astype(BF16), psel_ref[...])
        na_ref[sb, 0] = _pair_words(na)
        ca_ref[sb, 0] = _pair_words(ca)


def _pair_words(x):
    u = lax.bitcast_convert_type(x.astype(BF16).astype(F32), jnp.uint32)
    return u | (u >> 16)


def _packed_rows(words):
    one = pltpu.bitcast(jnp.broadcast_to(words, (BF16_SUBLANES // 2, LANES)), BF16)
    return jnp.concatenate([one] * (LANES // BF16_SUBLANES), axis=0)


def _gelu(x):
    return 0.5 * x * (1.0 + lax.erf(x * (0.5 ** 0.5)))


def _peer_dense_kernel(xn_ref, u_ref, vt_ref, r2_ref, e2_ref, na_ref, ca_ref, h_ref, o_ref,
                       acc_ref, hida_ref, hidb_ref, xu_ref, *, mt, eb):
    e = pl.program_id(1)
    n_eblk = pl.num_programs(1) - 1

    @pl.when(e == 0)
    def _():
        acc_ref[...] = jnp.zeros_like(acc_ref)
        hidb_ref[...] = jnp.zeros_like(hidb_ref)

    a0 = jnp.minimum(e, n_eblk - 1) * (eb // LANES)
    zero = jnp.zeros((LANES, LANES), BF16)
    nchunk = eb // PEER_CHUNK
    dch = acc_ref.shape[0] // nchunk

    def step(new_ref, old_ref):
        def v_matmul(j):
            acc_ref[j * dch:(j + 1) * dch, :] += jnp.dot(vt_ref[j * dch:(j + 1) * dch, :], old_ref[...],
                                                         preferred_element_type=F32)

        def u_matmul(j):
            rows = slice(j * PEER_CHUNK, (j + 1) * PEER_CHUNK)
            xu_ref[j % 2] = lax.dot_general(u_ref[rows, :], xn_ref[...], _NT,
                                            preferred_element_type=F32)

        def gate_tiles(j, sb):
            n_ablk = PEER_CHUNK // LANES
            gs = [zero] * n_ablk
            for hd in range(PEER_HEADS):
                r2 = pltpu.bitcast(r2_ref[sb, hd], BF16)
                e2 = pltpu.bitcast(e2_ref[sb, hd], BF16)
                for aa in range(n_ablk):
                    a = a0 + j * n_ablk + aa
                    n_row = _packed_rows(na_ref[sb, hd, pl.ds(a, 1), :])
                    c_row = _packed_rows(ca_ref[sb, hd, pl.ds(a, 1), :])
                    gs[aa] = gs[aa] + jnp.where(r2 < n_row, e2, zero) * c_row
            for aa in range(n_ablk):
                r0 = j * PEER_CHUNK + aa * LANES
                act = _gelu(xu_ref[j % 2, aa * LANES:(aa + 1) * LANES, sb * LANES:(sb + 1) * LANES])
                new_ref[r0:r0 + LANES, sb * LANES:(sb + 1) * LANES] = act.astype(BF16) * gs[aa]

        nsub = mt // LANES
        v_matmul(0)
        u_matmul(0)
        for j in range(nchunk):
            if j + 1 < nchunk:
                v_matmul(j + 1)
            for sb in range(nsub // 2):
                gate_tiles(j, sb)
            if j + 1 < nchunk:
                u_matmul(j + 1)
            for sb in range(nsub // 2, nsub):
                gate_tiles(j, sb)

    @pl.when(e % 2 == 0)
    def _():
        step(hida_ref, hidb_ref)

    @pl.when(e % 2 == 1)
    def _():
        step(hidb_ref, hida_ref)

    @pl.when(e == n_eblk)
    def _():
        o_ref[...] = h_ref[...] + acc_ref[...].T


def _peer_layer(h2, g_ffn, w_q, keys, u_tab, v_tab, mt, eb):
    n, d = h2.shape
    n_eblk = u_tab.shape[0] // eb
    nsub = mt // LANES
    half = PEER_NKEYS // 2
    rshape_w = jax.ShapeDtypeStruct((n // LANES, PEER_HEADS, PEER_NKEYS, LANES), jnp.uint32)
    rshape_p = jax.ShapeDtypeStruct((n // LANES, PEER_HEADS, half, LANES), jnp.uint32)
    rspec = pl.BlockSpec((nsub, 1, PEER_NKEYS, LANES), lambda i, hd: (i, hd, 0, 0))
    pspec = pl.BlockSpec((nsub, 1, half, LANES), lambda i, hd: (i, hd, 0, 0))
    psel = np.zeros((PEER_NKEYS, PEER_NKEYS), np.float32)
    psel[np.arange(half), 2 * np.arange(half)] = 1.0
    psel[half + np.arange(half), 2 * np.arange(half) + 1] = 1.0
    xn, r2, e2, na, ca = pl.pallas_call(
        functools.partial(_peer_route_kernel, mt=mt),
        grid=(n // mt, PEER_HEADS),
        in_specs=[pl.BlockSpec((mt, d), lambda i, hd: (i, 0)),
                  pl.BlockSpec((1, d), lambda i, hd: (0, 0)),
                  pl.BlockSpec((d, 2 * PEER_HALF), lambda i, hd: (0, hd)),
                  pl.BlockSpec((1, 2, PEER_NKEYS, PEER_HALF), lambda i, hd: (hd, 0, 0, 0)),
                  pl.BlockSpec((PEER_NKEYS, PEER_NKEYS), lambda i, hd: (0, 0))],
        out_specs=[pl.BlockSpec((mt, d), lambda i, hd: (i, 0)), pspec, pspec, rspec, rspec],
        out_shape=[jax.ShapeDtypeStruct((n, d), BF16), rshape_p, rshape_p, rshape_w, rshape_w],
        scratch_shapes=[pltpu.VMEM((mt, d), BF16), pltpu.VMEM((mt, 2 * PEER_HALF), BF16)],
        compiler_params=_cparams("parallel", "arbitrary"),
        name="peer_route",
    )(h2, g_ffn.reshape(1, d), w_q.astype(BF16), keys.astype(BF16), jnp.asarray(psel, BF16))

    dspec = pl.BlockSpec((nsub, PEER_HEADS, PEER_NKEYS, LANES), lambda i, e: (i, 0, 0, 0))
    dspec_p = pl.BlockSpec((nsub, PEER_HEADS, half, LANES), lambda i, e: (i, 0, 0, 0))
    return pl.pallas_call(
        functools.partial(_peer_dense_kernel, mt=mt, eb=eb),
        grid=(n // mt, n_eblk + 1),
        in_specs=[pl.BlockSpec((mt, d), lambda i, e: (i, 0)),
                  pl.BlockSpec((eb, d), lambda i, e: (jnp.minimum(e, n_eblk - 1), 0)),
                  pl.BlockSpec((d, eb), lambda i, e: (0, jnp.maximum(e - 1, 0))),
                  dspec_p, dspec_p, dspec, dspec,
                  pl.BlockSpec((mt, d), lambda i, e: (i, 0))],
        out_specs=pl.BlockSpec((mt, d), lambda i, e: (i, 0)),
        out_shape=jax.ShapeDtypeStruct((n, d), F32),
        scratch_shapes=[pltpu.VMEM((d, mt), F32), pltpu.VMEM((eb, mt), BF16),
                        pltpu.VMEM((eb, mt), BF16), pltpu.VMEM((2, PEER_CHUNK, mt), F32)],
        compiler_params=_cparams("parallel", "arbitrary"),
        name="peer_dense",
    )(xn, u_tab.astype(BF16), v_tab.T.astype(BF16), r2, e2, na, ca, h2)


def kernel(x, norm_mix, norm_ffn, norm_final, hgrn_lb_logits, hgrn_w_in, hgrn_gnorm, hgrn_w_out,
           fox_w_in, fox_b_f, fox_w_out, ssm_w_in, ssm_conv_w, ssm_conv_b, ssm_dt_bias, ssm_a_log,
           ssm_d, ssm_gnorm, ssm_w_out, peer_w_q, peer_keys, peer_u, peer_v):
    b, t, d = x.shape
    n = b * t
    depth = norm_mix.shape[0]
    h = x
    for i in range(depth):
        kind, j = i % 3, i // 3
        if kind == 0:
            proj = _rms_proj(h.reshape(n, d), norm_mix[i], hgrn_w_in[j].astype(BF16), F32, 512, 1024)
            h = _hgrn_layer(h, proj.reshape(b, t, 4 * d), hgrn_lb_logits, hgrn_gnorm[j],
                            hgrn_w_out[j].astype(BF16), i)
        elif kind == 1:
            h = _fox_layer(h, norm_mix[i], fox_w_in[j], fox_b_f[j], fox_w_out[j])
        else:
            h = _ssd_layer(h, norm_mix[i], ssm_w_in[j], ssm_conv_w[j], ssm_conv_b[j], ssm_dt_bias[j],
                           ssm_a_log[j], ssm_d[j], ssm_gnorm[j], ssm_w_out[j])
        h = _peer_layer(h.reshape(n, d), norm_ffn[i], peer_w_q[i], peer_keys[i], peer_u[i],
                        peer_v[i], min(512, n), 1024).reshape(b, t, d)
    return _final_norm(h.reshape(n, d), norm_final, 512).reshape(b, t, d)
```

```python
import functools
import math

import numpy as np
import jax
import jax.numpy as jnp
from jax import lax
from jax.experimental import pallas as pl
from jax.experimental.pallas import tpu as pltpu

F32 = jnp.float32
BF16 = jnp.bfloat16

NORM_EPS = 1e-6
LANES = 128
BF16_SUBLANES = 16
VMEM_LIMIT = 56 * 1024 * 1024

D_MODEL = 1024
HG_HEADS = 8
HG_DK = 128
GLA_CHUNK = 128
GLA_SUB = 16
FOX_HEADS = 16
FOX_DH = 64
FOX_TILE = 1024
SSM_DI = 2048
SSM_HEADS = 32
SSM_HEADDIM = 64
SSM_GROUPS = 8
SSM_STATE = 128
SSM_CONV = 4
SSM_CHUNK = 128
PEER_HEADS = 8
PEER_NKEYS = 128
PEER_TOPK = 16
PEER_HALF = 128
PEER_CHUNK = 256

_NT = (((1,), (1,)), ((), ()))
_TN = (((0,), (0,)), ((), ()))


def _cparams(*sem):
    return pltpu.CompilerParams(dimension_semantics=sem, vmem_limit_bytes=VMEM_LIMIT)


def _split3(x):
    hi = x.astype(BF16)
    r = x - hi.astype(F32)
    mid = r.astype(BF16)
    lo = (r - mid.astype(F32)).astype(BF16)
    return hi, mid, lo


def _dot01(m01, x):
    hi, mid, lo = _split3(x)
    d = lambda b: jnp.dot(m01, b, preferred_element_type=F32)
    return d(hi) + d(mid) + d(lo)


def _dot01_r(x, m01):
    hi, mid, lo = _split3(x)
    d = lambda a: jnp.dot(a, m01, preferred_element_type=F32)
    return d(hi) + d(mid) + d(lo)


def _sigmoid(x):
    return 1.0 / (1.0 + jnp.exp(-x))


def _silu(x):
    return x * _sigmoid(x)


def _rms(x, g):
    ms = jnp.mean(x * x, axis=-1, keepdims=True)
    return x * lax.rsqrt(ms + NORM_EPS) * g


def _rms_proj_kernel(x_ref, g_ref, w_ref, o_ref, xn_ref):
    @pl.when(pl.program_id(1) == 0)
    def _():
        xn_ref[...] = _rms(x_ref[...], g_ref[...]).astype(BF16)

    o_ref[...] = jnp.dot(xn_ref[...], w_ref[...], preferred_element_type=F32).astype(o_ref.dtype)


def _rms_proj(x2d, g, w_bf16, out_dtype, tm, tn):
    n, d = x2d.shape
    f = w_bf16.shape[1]
    assert n % tm == 0 and f % tn == 0
    return pl.pallas_call(
        _rms_proj_kernel,
        grid=(n // tm, f // tn),
        in_specs=[pl.BlockSpec((tm, d), lambda i, j: (i, 0)),
                  pl.BlockSpec((1, d), lambda i, j: (0, 0)),
                  pl.BlockSpec((d, tn), lambda i, j: (0, j))],
        out_specs=pl.BlockSpec((tm, tn), lambda i, j: (i, j)),
        out_shape=jax.ShapeDtypeStruct((n, f), out_dtype),
        scratch_shapes=[pltpu.VMEM((tm, d), BF16)],
        compiler_params=_cparams("parallel", "arbitrary"),
        name="rms_proj",
    )(x2d, g.reshape(1, d), w_bf16)


def _mm_res_kernel(x_ref, w_ref, r_ref, o_ref):
    o_ref[...] = r_ref[...] + jnp.dot(x_ref[...], w_ref[...], preferred_element_type=F32)


def _mm_res(x2d_bf16, w_bf16, res2d, tm):
    n, k = x2d_bf16.shape
    d = w_bf16.shape[1]
    return pl.pallas_call(
        _mm_res_kernel,
        grid=(n // tm,),
        in_specs=[pl.BlockSpec((tm, k), lambda i: (i, 0)),
                  pl.BlockSpec((k, d), lambda i: (0, 0)),
                  pl.BlockSpec((tm, d), lambda i: (i, 0))],
        out_specs=pl.BlockSpec((tm, d), lambda i: (i, 0)),
        out_shape=jax.ShapeDtypeStruct((n, d), F32),
        compiler_params=_cparams("parallel"),
        name="mm_res",
    )(x2d_bf16, w_bf16, res2d)


def _final_norm_kernel(x_ref, g_ref, o_ref):
    o_ref[...] = _rms(x_ref[...], g_ref[...])


def _final_norm(x2d, g, tm):
    n, d = x2d.shape
    return pl.pallas_call(
        _final_norm_kernel,
        grid=(n // tm,),
        in_specs=[pl.BlockSpec((tm, d), lambda i: (i, 0)),
                  pl.BlockSpec((1, d), lambda i: (0, 0))],
        out_specs=pl.BlockSpec((tm, d), lambda i: (i, 0)),
        out_shape=jax.ShapeDtypeStruct((n, d), F32),
        compiler_params=_cparams("parallel"),
        name="final_norm",
    )(x2d, g.reshape(1, d))


def _gla_tables(c, sub):
    t = np.arange(c)[:, None]
    j = np.arange(c)[None, :]
    mats = [(j <= t).astype(np.float32)]
    masks = []
    bk = c // 2
    while bk >= sub:
        blk = t // bk
        odd = (blk % 2) == 1
        mats.append(((odd & (j >= blk * bk) & (j <= t))
                     | ((~odd) & (j > t) & (j < (blk + 1) * bk))).astype(np.float32))
        masks.append((odd & ((j // bk) == blk - 1)).astype(np.float32))
        bk //= 2
    m0 = (t // sub) * sub
    mats.append(((j >= m0) & (j <= t)).astype(np.float32))
    masks.append((((t // sub) == (j // sub)) & (j <= t)).astype(np.float32))
    return np.concatenate(mats, axis=0), np.stack(masks, axis=0)


def _hgrn_kernel(proj_ref, h_ref, lbl_ref, gn_ref, wout_ref, dst_ref, msk_ref, o_ref,
                 st_ref, y_ref, *, layer, depth, c, nlev):
    @pl.when(pl.program_id(1) == 0)
    def _():
        st_ref[...] = jnp.zeros_like(st_ref)

    rows = [lbl_ref[j:j + 1, :] for j in range(depth)]
    mx = functools.reduce(jnp.maximum, rows)
    es = [jnp.exp(r - mx) for r in rows]
    tot = functools.reduce(lambda a, b: a + b, es)
    lb = jnp.zeros_like(mx)
    for j in range(1, layer + 1):
        lb = lb + es[j] / tot

    d = HG_HEADS * HG_DK
    q = proj_ref[0, :, 0:d] * (HG_DK ** -0.5)
    f = lb + (1.0 - lb) * _sigmoid(proj_ref[0, :, d:2 * d])
    k = 1.0 - f
    expo = _dot01(dst_ref[...], jnp.log(f))
    b_cum = expo[0:c]
    q_dec = (q * jnp.exp(b_cum)).astype(BF16)
    k_dec = (k * jnp.exp(b_cum[c - 1:c] - b_cum)).astype(BF16)
    dec_last = jnp.exp(b_cum[c - 1:c])
    qs, ks = [], []
    for l in range(nlev - 1):
        w = jnp.exp(expo[(1 + l) * c:(2 + l) * c])
        qs.append((q * w).astype(BF16))
        ks.append((k * w).astype(BF16))
    e_diag = expo[nlev * c:(nlev + 1) * c]
    qs.append((q * jnp.exp(e_diag)).astype(BF16))
    ks.append((k * jnp.exp(-e_diag)).astype(BF16))
    v = proj_ref[0, :, 2 * d:3 * d].astype(BF16)
    gate = proj_ref[0, :, 3 * d:4 * d]
    gn = gn_ref[...]
    for hd in range(HG_HEADS):
        sl = slice(hd * HG_DK, (hd + 1) * HG_DK)
        a = jnp.zeros((c, c), F32)
        for l in range(nlev):
            a = a + lax.dot_general(qs[l][:, sl], ks[l][:, sl], _NT,
                                    preferred_element_type=F32) * msk_ref[l]
        st = st_ref[hd]
        o = (jnp.dot(a.astype(BF16), v[:, sl], preferred_element_type=F32)
             + lax.dot_general(q_dec[:, sl], st.astype(BF16), _NT, preferred_element_type=F32))
        st_ref[hd] = st * dec_last[:, sl] + lax.dot_general(
            v[:, sl], k_dec[:, sl], _TN, preferred_element_type=F32)
        y_ref[:, sl] = (_rms(o, gn) * _silu(gate[:, sl])).astype(BF16)
    o_ref[0] = h_ref[0] + jnp.dot(y_ref[...], wout_ref[...], preferred_element_type=F32)


def _hgrn_layer(h, proj, lb_logits, g_norm, w_out_bf16, layer):
    b, t, d = h.shape
    c = GLA_CHUNK
    dst, msk = _gla_tables(c, GLA_SUB)
    nlev = msk.shape[0]
    depth = lb_logits.shape[0]
    kern = functools.partial(_hgrn_kernel, layer=layer, depth=depth, c=c, nlev=nlev)
    return pl.pallas_call(
        kern,
        grid=(b, t // c),
        in_specs=[pl.BlockSpec((1, c, 4 * d), lambda i, j: (i, j, 0)),
                  pl.BlockSpec((1, c, d), lambda i, j: (i, j, 0)),
                  pl.BlockSpec((depth, d), lambda i, j: (0, 0)),
                  pl.BlockSpec((1, HG_DK), lambda i, j: (0, 0)),
                  pl.BlockSpec((d, d), lambda i, j: (0, 0)),
                  pl.BlockSpec(dst.shape, lambda i, j: (0, 0)),
                  pl.BlockSpec(msk.shape, lambda i, j: (0, 0, 0))],
        out_specs=pl.BlockSpec((1, c, d), lambda i, j: (i, j, 0)),
        out_shape=jax.ShapeDtypeStruct((b, t, d), F32),
        scratch_shapes=[pltpu.VMEM((HG_HEADS, HG_DK, HG_DK), F32),
                        pltpu.VMEM((c, d), BF16)],
        compiler_params=_cparams("parallel", "arbitrary"),
        name="hgrn_gla",
    )(proj, h, lb_logits, g_norm.reshape(1, HG_DK), w_out_bf16,
      jnp.asarray(dst, BF16), jnp.asarray(msk, F32))


def _log_sigmoid(x):
    return jnp.minimum(x, 0.0) - jnp.log1p(jnp.exp(-jnp.abs(x)))


def _fox_cum_kernel(fl_ref, bf_ref, ltri_ref, place_ref, o_ref, carry_ref, *, tc):
    @pl.when(pl.program_id(1) == 0)
    def _():
        carry_ref[...] = jnp.zeros_like(carry_ref)

    lf = _log_sigmoid(fl_ref[0] + bf_ref[...])
    c = _dot01(ltri_ref[...], lf) + carry_ref[...]
    carry_ref[...] = c[tc - 1:tc, :]
    hi, mid, lo = _split3(-c)
    o_ref[0] = jnp.dot(jnp.concatenate([hi, mid, lo], axis=1), place_ref[...],
                       preferred_element_type=F32).astype(BF16)


def _fox_cumsum(f_logit, b_f_pad, tc):
    b, t, _ = f_logit.shape
    ltri = np.tril(np.ones((tc, tc), np.float32))
    place = np.zeros((3 * LANES, FOX_HEADS * LANES), np.float32)
    for hd in range(FOX_HEADS):
        for part in range(3):
            place[part * LANES + hd, hd * LANES + part] = 1.0
    return pl.pallas_call(
        functools.partial(_fox_cum_kernel, tc=tc),
        grid=(b, t // tc),
        in_specs=[pl.BlockSpec((1, tc, LANES), lambda i, j: (i, j, 0)),
                  pl.BlockSpec((1, LANES), lambda i, j: (0, 0)),
                  pl.BlockSpec((tc, tc), lambda i, j: (0, 0)),
                  pl.BlockSpec(place.shape, lambda i, j: (0, 0))],
        out_specs=pl.BlockSpec((1, tc, FOX_HEADS * LANES), lambda i, j: (i, j, 0)),
        out_shape=jax.ShapeDtypeStruct((b, t, FOX_HEADS * LANES), BF16),
        scratch_shapes=[pltpu.VMEM((1, LANES), F32)],
        compiler_params=_cparams("parallel", "arbitrary"),
        name="fox_cumsum",
    )(f_logit, b_f_pad, jnp.asarray(ltri, BF16), jnp.asarray(place, BF16))


def _fox_flash_kernel(q_ref, k_ref, cf_ref, vt_ref, o_ref, qp_ref, m_ref, l_ref, acc_ref, *, tq, tk):
    qi = pl.program_id(2)
    ki = pl.program_id(3)
    lane = lax.broadcasted_iota(jnp.int32, (1, LANES), 1)

    @pl.when(ki == 0)
    def _():
        m_ref[...] = jnp.full_like(m_ref, -1e30)
        l_ref[...] = jnp.zeros_like(l_ref)
        acc_ref[...] = jnp.zeros_like(acc_ref)
        q = q_ref[0] * (FOX_DH ** -0.5)
        ones3 = jnp.broadcast_to(jnp.where(lane < 3, 1.0, 0.0).astype(BF16), (tq, LANES))
        for hh in range(2):
            qh = jnp.where((lane < FOX_DH) == (hh == 0), q, jnp.zeros_like(q))
            qp_ref[hh] = jnp.concatenate([qh, ones3], axis=1)

    def body(masked):
        k = k_ref[0]
        st = []
        for hh in range(2):
            kp = jnp.concatenate([k, cf_ref[0, :, hh * LANES:(hh + 1) * LANES]], axis=1)
            st.append(lax.dot_general(kp, qp_ref[hh], _NT, preferred_element_type=F32))
        if masked:
            key_pos = lax.broadcasted_iota(jnp.int32, (tk, tq), 0)
            qry_pos = lax.broadcasted_iota(jnp.int32, (tk, tq), 1)
            causal = key_pos <= qry_pos
        for hh in range(2):
            s = jnp.where(causal, st[hh], -1e30) if masked else st[hh]
            m_prev = m_ref[hh]
            m_new = jnp.maximum(m_prev, jnp.max(s, axis=0, keepdims=True))
            alpha = jnp.exp(m_prev - m_new)
            p = jnp.exp(s - m_new)
            l_ref[hh] = alpha * l_ref[hh] + jnp.sum(p, axis=0, keepdims=True)
            acc_ref[hh] = alpha * acc_ref[hh] + jnp.dot(
                vt_ref[0, hh * FOX_DH:(hh + 1) * FOX_DH, :], p.astype(BF16), preferred_element_type=F32)
            m_ref[hh] = m_new

    @pl.when(ki < qi)
    def _():
        body(False)

    @pl.when(ki == qi)
    def _():
        body(True)
        o_ref[0] = jnp.concatenate([acc_ref[0] / l_ref[0], acc_ref[1] / l_ref[1]], axis=0).astype(o_ref.dtype)


def _fox_flash(qk, cfeat, v_t, tq):
    b, t, _ = qk.shape
    npair = FOX_HEADS // 2
    tk = tq
    kv = lambda qi, ki: jnp.minimum(ki, qi)
    return pl.pallas_call(
        functools.partial(_fox_flash_kernel, tq=tq, tk=tk),
        grid=(b, npair, t // tq, t // tk),
        in_specs=[pl.BlockSpec((1, tq, LANES), lambda i, hp, qi, ki: (i, qi, hp)),
                  pl.BlockSpec((1, tk, LANES), lambda i, hp, qi, ki: (i, kv(qi, ki), npair + hp)),
                  pl.BlockSpec((1, tk, 2 * LANES), lambda i, hp, qi, ki: (i, kv(qi, ki), hp)),
                  pl.BlockSpec((1, LANES, tk), lambda i, hp, qi, ki: (i, hp, kv(qi, ki)))],
        out_specs=pl.BlockSpec((1, LANES, tq), lambda i, hp, qi, ki: (i, hp, qi)),
        out_shape=jax.ShapeDtypeStruct((b, D_MODEL, t), BF16),
        scratch_shapes=[pltpu.VMEM((2, tq, 2 * LANES), BF16),
                        pltpu.VMEM((2, 1, tq), F32),
                        pltpu.VMEM((2, 1, tq), F32),
                        pltpu.VMEM((2, FOX_DH, tq), F32)],
        compiler_params=_cparams("parallel", "parallel", "parallel", "arbitrary"),
        name="fox_flash",
    )(qk, qk, cfeat, v_t)


def _rms_proj_t_kernel(x_ref, g_ref, wt_ref, o_ref):
    xn = _rms(x_ref[0], g_ref[...]).astype(BF16)
    o_ref[0] = lax.dot_general(wt_ref[...], xn, _NT, preferred_element_type=F32).astype(o_ref.dtype)


def _rms_proj_t(x, g, wt_bf16, tm):
    b, t, d = x.shape
    f = wt_bf16.shape[0]
    return pl.pallas_call(
        _rms_proj_t_kernel,
        grid=(b, t // tm),
        in_specs=[pl.BlockSpec((1, tm, d), lambda i, j: (i, j, 0)),
                  pl.BlockSpec((1, d), lambda i, j: (0, 0)),
                  pl.BlockSpec((f, d), lambda i, j: (0, 0))],
        out_specs=pl.BlockSpec((1, f, tm), lambda i, j: (i, 0, j)),
        out_shape=jax.ShapeDtypeStruct((b, f, t), BF16),
        compiler_params=_cparams("parallel", "parallel"),
        name="rms_proj_t",
    )(x, g.reshape(1, d), wt_bf16)


def _mm_t_res_kernel(xt_ref, w_ref, r_ref, o_ref):
    o_ref[0] = r_ref[0] + lax.dot_general(xt_ref[0], w_ref[...], _TN, preferred_element_type=F32)


def _mm_t_res(x_t, w_bf16, res, tm):
    b, k, t = x_t.shape
    d = w_bf16.shape[1]
    return pl.pallas_call(
        _mm_t_res_kernel,
        grid=(b, t // tm),
        in_specs=[pl.BlockSpec((1, k, tm), lambda i, j: (i, 0, j)),
                  pl.BlockSpec((k, d), lambda i, j: (0, 0)),
                  pl.BlockSpec((1, tm, d), lambda i, j: (i, j, 0))],
        out_specs=pl.BlockSpec((1, tm, d), lambda i, j: (i, j, 0)),
        out_shape=jax.ShapeDtypeStruct((b, t, d), F32),
        compiler_params=_cparams("parallel", "parallel"),
        name="mm_t_res",
    )(x_t, w_bf16, res)


def _fox_layer(h, g_mix, w_in, b_f, w_out):
    b, t, d = h.shape
    n = b * t
    h2 = h.reshape(n, d)
    w_qk = w_in[:, :2 * d].astype(BF16)
    w_vt = w_in[:, 2 * d:3 * d].T.astype(BF16)
    w_f = jnp.pad(w_in[:, 3 * d:], ((0, 0), (0, LANES - FOX_HEADS))).astype(BF16)
    qk = _rms_proj(h2, g_mix, w_qk, BF16, 512, 1024).reshape(b, t, 2 * d)
    v_t = _rms_proj_t(h, g_mix, w_vt, min(512, t))
    fl = _rms_proj(h2, g_mix, w_f, F32, 512, LANES).reshape(b, t, LANES)
    b_f_pad = jnp.pad(b_f, (0, LANES - FOX_HEADS)).reshape(1, LANES)
    cfeat = _fox_cumsum(fl, b_f_pad, min(512, t))
    o_t = _fox_flash(qk, cfeat, v_t, min(FOX_TILE, t))
    return _mm_t_res(o_t, w_out.astype(BF16), h, min(512, t))


def _softplus(x):
    return jnp.maximum(x, 0.0) + jnp.log1p(jnp.exp(-jnp.abs(x)))


def _ssd_kernel(p_ref, h_ref, cw_ref, cb_ref, dtb_ref, alog_ref, dex_ref, gn_ref, wout_ref,
                ltri_ref, eexp_ref, o_ref, buf_ref, st_ref, y_ref, *, c):
    di = SSM_DI
    gn_w = SSM_GROUPS * SSM_STATE
    gw = di // SSM_GROUPS

    @pl.when(pl.program_id(1) == 0)
    def _():
        buf_ref[0:8, :] = jnp.zeros((8, buf_ref.shape[1]), F32)
        st_ref[...] = jnp.zeros_like(st_ref)

    buf_ref[8:8 + c, :] = p_ref[0, :, di:di + di + 2 * gn_w]
    conv = cb_ref[...]
    for kk in range(SSM_CONV):
        conv = conv + cw_ref[kk:kk + 1, :] * buf_ref[5 + kk:5 + kk + c, :]
    buf_ref[0:8, :] = buf_ref[c:c + 8, :]
    xbc = _silu(conv)
    xs = xbc[:, :di]
    bm = xbc[:, di:di + gn_w]
    cm = xbc[:, di + gn_w:]

    dt = _softplus(p_ref[0, :, 2 * di + 2 * gn_w:] + dtb_ref[...])
    da = dt * (-jnp.exp(alog_ref[...]))
    cum = _dot01(ltri_ref[...], da)
    cum_t = cum.T
    ex = _dot01_r(jnp.concatenate([cum, dt], axis=0), eexp_ref[...])
    cum_e = ex[:c]
    dt_e = ex[c:]
    last = cum_e[c - 1:c]
    expcum = jnp.exp(cum_e)
    declast = jnp.exp(last)
    xdt = xs * dt_e
    xdt_b = xdt.astype(BF16)
    xdec_b = (xdt * jnp.exp(last - cum_e)).astype(BF16)

    row = lax.broadcasted_iota(jnp.int32, (c, c), 0)
    col = lax.broadcasted_iota(jnp.int32, (c, c), 1)
    tril = row >= col
    lane = lax.broadcasted_iota(jnp.int32, (1, LANES), 1)
    for g in range(SSM_GROUPS):
        cg = cm[:, g * SSM_STATE:(g + 1) * SSM_STATE].astype(BF16)
        bg32 = bm[:, g * SSM_STATE:(g + 1) * SSM_STATE]
        cb = lax.dot_general(cg, bg32.astype(BF16), _NT, preferred_element_type=F32)
        st = st_ref[g]
        y_inter = (jnp.dot(cg, st.astype(BF16), preferred_element_type=F32)
                   * expcum[:, g * gw:(g + 1) * gw])
        for pr in range(2):
            base = g * gw + pr * LANES
            xpair = xdt_b[:, base:base + LANES]
            outs = []
            for hh in range(2):
                hidx = g * 4 + pr * 2 + hh
                diff = cum[:, hidx:hidx + 1] - cum_t[hidx:hidx + 1, :]
                lmat = jnp.exp(jnp.where(tril, diff, -1e30))
                outs.append(jnp.dot((cb * lmat).astype(BF16), xpair, preferred_element_type=F32))
            y_intra = jnp.where(lane < SSM_HEADDIM, outs[0], outs[1])
            y_ref[:, base:base + LANES] = (y_intra + y_inter[:, pr * LANES:(pr + 1) * LANES]
                                           + xs[:, base:base + LANES] * dex_ref[:, base:base + LANES])
        st_ref[g] = st * declast[:, g * gw:(g + 1) * gw] + jnp.dot(
            bg32.T.astype(BF16), xdec_b[:, g * gw:(g + 1) * gw], preferred_element_type=F32)

    y = y_ref[...] * _silu(p_ref[0, :, 0:di])
    for g in range(SSM_GROUPS):
        sl = slice(g * gw, (g + 1) * gw)
        y_ref[:, sl] = _rms(y[:, sl], gn_ref[:, sl])
    o_ref[0] = h_ref[0] + jnp.dot(y_ref[...].astype(BF16), wout_ref[...], preferred_element_type=F32)


def _ssd_layer(h, g_mix, w_in, conv_w, conv_b, dt_bias, a_log, d_skip, g_norm, w_out):
    b, t, d = h.shape
    n = b * t
    c = min(SSM_CHUNK, t)
    di = SSM_DI
    cdim = di + 2 * SSM_GROUPS * SSM_STATE
    pad_h = LANES - SSM_HEADS
    w_all = jnp.pad(w_in, ((0, 0), (0, pad_h))).astype(BF16)
    fdim = w_all.shape[1]
    proj = _rms_proj(h.reshape(n, d), g_mix, w_all, F32, 512, 896).reshape(b, t, fdim)
    eexp = np.zeros((LANES, di), np.float32)
    for hd in range(SSM_HEADS):
        eexp[hd, hd * SSM_HEADDIM:(hd + 1) * SSM_HEADDIM] = 1.0
    ltri = np.tril(np.ones((c, c), np.float32))
    full = lambda shape: pl.BlockSpec(shape, lambda i, j: tuple(0 for _ in shape))
    return pl.pallas_call(
        functools.partial(_ssd_kernel, c=c),
        grid=(b, t // c),
        in_specs=[pl.BlockSpec((1, c, fdim), lambda i, j: (i, j, 0)),
                  pl.BlockSpec((1, c, d), lambda i, j: (i, j, 0)),
                  full((SSM_CONV, cdim)), full((1, cdim)), full((1, LANES)), full((1, LANES)),
                  full((1, di)), full((1, di)), full((di, d)), full((c, c)), full((LANES, di))],
        out_specs=pl.BlockSpec((1, c, d), lambda i, j: (i, j, 0)),
        out_shape=jax.ShapeDtypeStruct((b, t, d), F32),
        scratch_shapes=[pltpu.VMEM((c + 8, cdim), F32),
                        pltpu.VMEM((SSM_GROUPS, SSM_STATE, di // SSM_GROUPS), F32),
                        pltpu.VMEM((c, di), F32)],
        compiler_params=_cparams("parallel", "arbitrary"),
        name="ssd_scan",
    )(proj, h, conv_w, conv_b.reshape(1, cdim),
      jnp.pad(dt_bias, (0, pad_h)).reshape(1, LANES), jnp.pad(a_log, (0, pad_h)).reshape(1, LANES),
      jnp.repeat(d_skip, SSM_HEADDIM).reshape(1, di), g_norm.reshape(1, di), w_out.astype(BF16),
      jnp.asarray(ltri, BF16), jnp.asarray(eexp, BF16))


def _stack_rows(rows):
    iota16 = lax.broadcasted_iota(jnp.int32, (PEER_TOPK, LANES), 0)
    out = jnp.zeros((PEER_TOPK, LANES), F32)
    for it, r in enumerate(rows):
        out = jnp.where(iota16 == it, r, out)
    return out


def _top16(s, exact):
    nrows = s.shape[0]
    iota = lax.broadcasted_iota(jnp.int32, (nrows, LANES), 0).astype(F32)
    rank = jnp.full((nrows, LANES), float(PEER_TOPK), F32)
    vals = []
    for it in range(PEER_TOPK):
        m = jnp.max(s, axis=0, keepdims=True)
        sel = s == m
        if exact:
            sel = iota == jnp.min(jnp.where(sel, iota, float(nrows)), axis=0, keepdims=True)
        rank = jnp.where(sel, float(it), rank)
        s = jnp.where(sel, -jnp.inf, s)
        vals.append(m)
    return _stack_rows(vals), rank


def _top16_all(scores):
    fast = [_top16(s, False) for s in scores]
    picked = jnp.zeros((1, LANES), F32)
    for _, r in fast:
        picked = picked + jnp.sum(jnp.where(r < float(PEER_TOPK), 1.0, 0.0), axis=0, keepdims=True)
    tied = jnp.max(jnp.abs(picked - float(PEER_TOPK * len(scores)))) > 0.0
    flat = lax.cond(tied,
                    lambda: tuple(x for s in scores for x in _top16(s, True)),
                    lambda: tuple(x for vr in fast for x in vr))
    return [(flat[2 * i], flat[2 * i + 1]) for i in range(len(scores))]


def _row_pair_words(x, psel):
    y = lax.bitcast_convert_type(jnp.dot(psel, x, preferred_element_type=F32), jnp.uint32)
    half = x.shape[0] // 2
    return (y[:half] >> 16) | y[half:]


def _peer_route_kernel(h_ref, g_ref, wq_ref, keys_ref, psel_ref, xn_ref, r2_ref, e2_ref, na_ref, ca_ref,
                       xs_ref, q_ref, *, mt):
    @pl.when(pl.program_id(1) == 0)
    def _():
        xn = _rms(h_ref[...], g_ref[...]).astype(BF16)
        xs_ref[...] = xn
        xn_ref[...] = xn

    q_ref[...] = jnp.dot(xs_ref[...], wq_ref[...], preferred_element_type=F32).astype(BF16)
    k1 = keys_ref[0, 0]
    k2 = keys_ref[0, 1]

    nsub = mt // LANES
    s1s, s2s = [], []
    for sb in range(nsub):
        qs = q_ref[sb * LANES:(sb + 1) * LANES, :]
        s1s.append(lax.dot_general(k1, qs[:, :PEER_HALF], _NT, preferred_element_type=F32))
        s2s.append(lax.dot_general(k2, qs[:, PEER_HALF:], _NT, preferred_element_type=F32))
    tops = _top16_all(s1s + s2s)
    iota16 = lax.broadcasted_iota(jnp.int32, (PEER_TOPK, LANES), 0).astype(F32)

    cnts = [jnp.zeros((PEER_TOPK, LANES), F32) for _ in range(nsub)]
    fronts = [tops[sb][0] + tops[nsub + sb][0][0:1] for sb in range(nsub)]
    for _ in range(PEER_TOPK):
        for sb in range(nsub):
            sv1, sv2, front, cnt = tops[sb][0], tops[nsub + sb][0], fronts[sb], cnts[sb]
            m = jnp.max(front, axis=0, keepdims=True)
            idx = jnp.min(jnp.where(front == m, iota16, float(PEER_TOPK)), axis=0, keepdims=True)
            sel = iota16 == idx
            cnt = jnp.where(sel, cnt + 1.0, cnt)
            n_sel = jnp.max(jnp.where(sel, cnt, 0.0), axis=0, keepdims=True)
            nxt = jnp.max(jnp.where(iota16 == n_sel, sv2, -jnp.inf), axis=0, keepdims=True)
            fronts[sb] = jnp.where(sel, sv1 + nxt, front)
            cnts[sb] = cnt

    for sb in range(nsub):
        sv1, rank1 = tops[sb]
        sv2, rank2 = tops[nsub + sb]
        cnt = cnts[sb]
        s2 = s2s[sb]
        v2 = [sv2[j:j + 1] for j in range(PEER_TOPK)]
        e2r = jnp.exp(sv2 - v2[0])
        pref = jnp.zeros((1, LANES), F32)
        pn = jnp.zeros((PEER_TOPK, LANES), F32)
        for j in range(PEER_TOPK):
            pref = pref + e2r[j:j + 1]
            pn = jnp.where(cnt == float(j + 1), pref, pn)
        e1 = jnp.exp(sv1 - sv1[0:1])
        z = jnp.sum(e1 * pn, axis=0, keepdims=True)
        cw = e1 * (1.0 / z)
        na = jnp.zeros((PEER_NKEYS, LANES), F32)
        ca = jnp.zeros((PEER_NKEYS, LANES), F32)
        for i in range(PEER_TOPK):
            hit = rank1 == float(i)
            na = jnp.where(hit, cnt[i:i + 1], na)
            ca = jnp.where(hit, cw[i:i + 1], ca)
        r2_ref[sb, 0] = _row_pair_words(rank2.astype(BF16), psel_ref[...])
        e2_ref[sb, 0] = _row_pair_words(jnp.exp(s2 - v2[0]).astype(BF16), psel_ref[...])
        na_ref[sb, 0] = _pair_words(na)
        ca_ref[sb, 0] = _pair_words(ca)


def _pair_words(x):
    u = lax.bitcast_convert_type(x.astype(BF16).astype(F32), jnp.uint32)
    return u | (u >> 16)


def _packed_rows(words):
    one = pltpu.bitcast(jnp.broadcast_to(words, (BF16_SUBLANES // 2, LANES)), BF16)
    return jnp.concatenate([one] * (LANES // BF16_SUBLANES), axis=0)


def _gelu(x):
    return 0.5 * x * (1.0 + lax.erf(x * (0.5 ** 0.5)))


def _peer_dense_kernel(xn_ref, u_ref, vt_ref, r2_ref, e2_ref, na_ref, ca_ref, h_ref, o_ref,
                       acc_ref, hida_ref, hidb_ref, xua_ref, xub_ref, *, mt, eb):
    e = pl.program_id(1)
    n_eblk = pl.num_programs(1) - 2

    @pl.when(e == 0)
    def _():
        acc_ref[...] = jnp.zeros_like(acc_ref)
        hida_ref[...] = jnp.zeros_like(hida_ref)
        hidb_ref[...] = jnp.zeros_like(hidb_ref)
        xub_ref[...] = jnp.zeros_like(xub_ref)

    a0 = jnp.clip(e - 1, 0, n_eblk - 1) * (eb // LANES)
    zero = jnp.zeros((LANES, LANES), BF16)
    nchunk = eb // PEER_CHUNK
    dch = acc_ref.shape[0] // nchunk

    def step(xu_new, xu_old, hid_new, hid_old):
        def v_matmul(j):
            acc_ref[j * dch:(j + 1) * dch, :] += jnp.dot(vt_ref[j * dch:(j + 1) * dch, :], hid_old[...],
                                                         preferred_element_type=F32)

        def u_matmul(j):
            rows = slice(j * PEER_CHUNK, (j + 1) * PEER_CHUNK)
            xu_new[rows, :] = lax.dot_general(u_ref[rows, :], xn_ref[...], _NT,
                                              preferred_element_type=F32)

        def gate_tile(ablk, sb):
            gs = zero
            for hd in range(PEER_HEADS):
                r2 = pltpu.bitcast(r2_ref[sb, hd], BF16)
                e2 = pltpu.bitcast(e2_ref[sb, hd], BF16)
                n_row = _packed_rows(na_ref[sb, hd, pl.ds(a0 + ablk, 1), :])
                c_row = _packed_rows(ca_ref[sb, hd, pl.ds(a0 + ablk, 1), :])
                gs = gs + jnp.where(r2 < n_row, e2, zero) * c_row
            r0 = ablk * LANES
            act = _gelu(xu_old[r0:r0 + LANES, sb * LANES:(sb + 1) * LANES])
            hid_new[r0:r0 + LANES, sb * LANES:(sb + 1) * LANES] = act.astype(BF16) * gs

        nsub = mt // LANES
        n_ablk = PEER_CHUNK // LANES
        for j in range(nchunk):
            v_matmul(j)
            for sb in range(nsub):
                gate_tile(j * n_ablk, sb)
            u_matmul(j)
            for sb in range(nsub):
                gate_tile(j * n_ablk + 1, sb)

    @pl.when(e % 2 == 0)
    def _():
        step(xua_ref, xub_ref, hidb_ref, hida_ref)

    @pl.when(e % 2 == 1)
    def _():
        step(xub_ref, xua_ref, hida_ref, hidb_ref)

    @pl.when(e == n_eblk + 1)
    def _():
        o_ref[...] = h_ref[...] + acc_ref[...].T


def _peer_layer(h2, g_ffn, w_q, keys, u_tab, v_tab, mt, eb):
    n, d = h2.shape
    n_eblk = u_tab.shape[0] // eb
    nsub = mt // LANES
    half = PEER_NKEYS // 2
    rshape_w = jax.ShapeDtypeStruct((n // LANES, PEER_HEADS, PEER_NKEYS, LANES), jnp.uint32)
    rshape_p = jax.ShapeDtypeStruct((n // LANES, PEER_HEADS, half, LANES), jnp.uint32)
    rspec = pl.BlockSpec((nsub, 1, PEER_NKEYS, LANES), lambda i, hd: (i, hd, 0, 0))
    pspec = pl.BlockSpec((nsub, 1, half, LANES), lambda i, hd: (i, hd, 0, 0))
    psel = np.zeros((PEER_NKEYS, PEER_NKEYS), np.float32)
    psel[np.arange(half), 2 * np.arange(half)] = 1.0
    psel[half + np.arange(half), 2 * np.arange(half) + 1] = 1.0
    xn, r2, e2, na, ca = pl.pallas_call(
        functools.partial(_peer_route_kernel, mt=mt),
        grid=(n // mt, PEER_HEADS),
        in_specs=[pl.BlockSpec((mt, d), lambda i, hd: (i, 0)),
                  pl.BlockSpec((1, d), lambda i, hd: (0, 0)),
                  pl.BlockSpec((d, 2 * PEER_HALF), lambda i, hd: (0, hd)),
                  pl.BlockSpec((1, 2, PEER_NKEYS, PEER_HALF), lambda i, hd: (hd, 0, 0, 0)),
                  pl.BlockSpec((PEER_NKEYS, PEER_NKEYS), lambda i, hd: (0, 0))],
        out_specs=[pl.BlockSpec((mt, d), lambda i, hd: (i, 0)), pspec, pspec, rspec, rspec],
        out_shape=[jax.ShapeDtypeStruct((n, d), BF16), rshape_p, rshape_p, rshape_w, rshape_w],
        scratch_shapes=[pltpu.VMEM((mt, d), BF16), pltpu.VMEM((mt, 2 * PEER_HALF), BF16)],
        compiler_params=_cparams("parallel", "arbitrary"),
        name="peer_route",
    )(h2, g_ffn.reshape(1, d), w_q.astype(BF16), keys.astype(BF16), jnp.asarray(psel, BF16))

    dspec = pl.BlockSpec((nsub, PEER_HEADS, PEER_NKEYS, LANES), lambda i, e: (i, 0, 0, 0))
    dspec_p = pl.BlockSpec((nsub, PEER_HEADS, half, LANES), lambda i, e: (i, 0, 0, 0))
    return pl.pallas_call(
        functools.partial(_peer_dense_kernel, mt=mt, eb=eb),
        grid=(n // mt, n_eblk + 2),
        in_specs=[pl.BlockSpec((mt, d), lambda i, e: (i, 0)),
                  pl.BlockSpec((eb, d), lambda i, e: (jnp.minimum(e, n_eblk - 1), 0)),
                  pl.BlockSpec((d, eb), lambda i, e: (0, jnp.clip(e - 2, 0, n_eblk - 1))),
                  dspec_p, dspec_p, dspec, dspec,
                  pl.BlockSpec((mt, d), lambda i, e: (i, 0))],
        out_specs=pl.BlockSpec((mt, d), lambda i, e: (i, 0)),
        out_shape=jax.ShapeDtypeStruct((n, d), F32),
        scratch_shapes=[pltpu.VMEM((d, mt), F32), pltpu.VMEM((eb, mt), BF16),
                        pltpu.VMEM((eb, mt), BF16), pltpu.VMEM((eb, mt), F32), pltpu.VMEM((eb, mt), F32)],
        compiler_params=_cparams("parallel", "arbitrary"),
        name="peer_dense",
    )(xn, u_tab.astype(BF16), v_tab.T.astype(BF16), r2, e2, na, ca, h2)


def kernel(x, norm_mix, norm_ffn, norm_final, hgrn_lb_logits, hgrn_w_in, hgrn_gnorm, hgrn_w_out,
           fox_w_in, fox_b_f, fox_w_out, ssm_w_in, ssm_conv_w, ssm_conv_b, ssm_dt_bias, ssm_a_log,
           ssm_d, ssm_gnorm, ssm_w_out, peer_w_q, peer_keys, peer_u, peer_v):
    b, t, d = x.shape
    n = b * t
    depth = norm_mix.shape[0]
    h = x
    for i in range(depth):
        kind, j = i % 3, i // 3
        if kind == 0:
            proj = _rms_proj(h.reshape(n, d), norm_mix[i], hgrn_w_in[j].astype(BF16), F32, 512, 1024)
            h = _hgrn_layer(h, proj.reshape(b, t, 4 * d), hgrn_lb_logits, hgrn_gnorm[j],
                            hgrn_w_out[j].astype(BF16), i)
        elif kind == 1:
            h = _fox_layer(h, norm_mix[i], fox_w_in[j], fox_b_f[j], fox_w_out[j])
        else:
            h = _ssd_layer(h, norm_mix[i], ssm_w_in[j], ssm_conv_w[j], ssm_conv_b[j], ssm_dt_bias[j],
                           ssm_a_log[j], ssm_d[j], ssm_gnorm[j], ssm_w_out[j])
        h = _peer_layer(h.reshape(n, d), norm_ffn[i], peer_w_q[i], peer_keys[i], peer_u[i],
                        peer_v[i], min(512, n), 1024).reshape(b, t, d)
    return _final_norm(h.reshape(n, d), norm_final, 512).reshape(b, t, d)
```

```python
import functools
import math

import numpy as np
import jax
import jax.numpy as jnp
from jax import lax
from jax.experimental import pallas as pl
from jax.experimental.pallas import tpu as pltpu

F32 = jnp.float32
BF16 = jnp.bfloat16

NORM_EPS = 1e-6
LANES = 128
BF16_SUBLANES = 16
VMEM_LIMIT = 56 * 1024 * 1024

D_MODEL = 1024
HG_HEADS = 8
HG_DK = 128
GLA_CHUNK = 128
GLA_SUB = 16
FOX_HEADS = 16
FOX_DH = 64
FOX_TILE = 1024
SSM_DI = 2048
SSM_HEADS = 32
SSM_HEADDIM = 64
SSM_GROUPS = 8
SSM_STATE = 128
SSM_CONV = 4
SSM_CHUNK = 128
PEER_HEADS = 8
PEER_NKEYS = 128
PEER_TOPK = 16
PEER_HALF = 128
PEER_CHUNK = 256

_NT = (((1,), (1,)), ((), ()))
_TN = (((0,), (0,)), ((), ()))


def _cparams(*sem):
    return pltpu.CompilerParams(dimension_semantics=sem, vmem_limit_bytes=VMEM_LIMIT)


def _split3(x):
    hi = x.astype(BF16)
    r = x - hi.astype(F32)
    mid = r.astype(BF16)
    lo = (r - mid.astype(F32)).astype(BF16)
    return hi, mid, lo


def _dot01(m01, x):
    hi, mid, lo = _split3(x)
    d = lambda b: jnp.dot(m01, b, preferred_element_type=F32)
    return d(hi) + d(mid) + d(lo)


def _dot01_r(x, m01):
    hi, mid, lo = _split3(x)
    d = lambda a: jnp.dot(a, m01, preferred_element_type=F32)
    return d(hi) + d(mid) + d(lo)


def _sigmoid(x):
    return 1.0 / (1.0 + jnp.exp(-x))


def _silu(x):
    return x * _sigmoid(x)


def _rms(x, g):
    ms = jnp.mean(x * x, axis=-1, keepdims=True)
    return x * lax.rsqrt(ms + NORM_EPS) * g


def _rms_proj_kernel(x_ref, g_ref, w_ref, o_ref, xn_ref):
    @pl.when(pl.program_id(1) == 0)
    def _():
        xn_ref[...] = _rms(x_ref[...], g_ref[...]).astype(BF16)

    o_ref[...] = jnp.dot(xn_ref[...], w_ref[...], preferred_element_type=F32).astype(o_ref.dtype)


def _rms_proj(x2d, g, w_bf16, out_dtype, tm, tn):
    n, d = x2d.shape
    f = w_bf16.shape[1]
    assert n % tm == 0 and f % tn == 0
    return pl.pallas_call(
        _rms_proj_kernel,
        grid=(n // tm, f // tn),
        in_specs=[pl.BlockSpec((tm, d), lambda i, j: (i, 0)),
                  pl.BlockSpec((1, d), lambda i, j: (0, 0)),
                  pl.BlockSpec((d, tn), lambda i, j: (0, j))],
        out_specs=pl.BlockSpec((tm, tn), lambda i, j: (i, j)),
        out_shape=jax.ShapeDtypeStruct((n, f), out_dtype),
        scratch_shapes=[pltpu.VMEM((tm, d), BF16)],
        compiler_params=_cparams("parallel", "arbitrary"),
        name="rms_proj",
    )(x2d, g.reshape(1, d), w_bf16)


def _mm_res_kernel(x_ref, w_ref, r_ref, o_ref):
    o_ref[...] = r_ref[...] + jnp.dot(x_ref[...], w_ref[...], preferred_element_type=F32)


def _mm_res(x2d_bf16, w_bf16, res2d, tm):
    n, k = x2d_bf16.shape
    d = w_bf16.shape[1]
    return pl.pallas_call(
        _mm_res_kernel,
        grid=(n // tm,),
        in_specs=[pl.BlockSpec((tm, k), lambda i: (i, 0)),
                  pl.BlockSpec((k, d), lambda i: (0, 0)),
                  pl.BlockSpec((tm, d), lambda i: (i, 0))],
        out_specs=pl.BlockSpec((tm, d), lambda i: (i, 0)),
        out_shape=jax.ShapeDtypeStruct((n, d), F32),
        compiler_params=_cparams("parallel"),
        name="mm_res",
    )(x2d_bf16, w_bf16, res2d)


def _final_norm_kernel(x_ref, g_ref, o_ref):
    o_ref[...] = _rms(x_ref[...], g_ref[...])


def _final_norm(x2d, g, tm):
    n, d = x2d.shape
    return pl.pallas_call(
        _final_norm_kernel,
        grid=(n // tm,),
        in_specs=[pl.BlockSpec((tm, d), lambda i: (i, 0)),
                  pl.BlockSpec((1, d), lambda i: (0, 0))],
        out_specs=pl.BlockSpec((tm, d), lambda i: (i, 0)),
        out_shape=jax.ShapeDtypeStruct((n, d), F32),
        compiler_params=_cparams("parallel"),
        name="final_norm",
    )(x2d, g.reshape(1, d))


def _gla_tables(c, sub):
    t = np.arange(c)[:, None]
    j = np.arange(c)[None, :]
    mats = [(j <= t).astype(np.float32)]
    masks = []
    bk = c // 2
    while bk >= sub:
        blk = t // bk
        odd = (blk % 2) == 1
        mats.append(((odd & (j >= blk * bk) & (j <= t))
                     | ((~odd) & (j > t) & (j < (blk + 1) * bk))).astype(np.float32))
        masks.append((odd & ((j // bk) == blk - 1)).astype(np.float32))
        bk //= 2
    m0 = (t // sub) * sub
    mats.append(((j >= m0) & (j <= t)).astype(np.float32))
    masks.append((((t // sub) == (j // sub)) & (j <= t)).astype(np.float32))
    return np.concatenate(mats, axis=0), np.stack(masks, axis=0)


def _hgrn_kernel(proj_ref, h_ref, lbl_ref, gn_ref, wout_ref, dst_ref, msk_ref, o_ref,
                 st_ref, y_ref, *, layer, depth, c, nlev):
    @pl.when(pl.program_id(1) == 0)
    def _():
        st_ref[...] = jnp.zeros_like(st_ref)

    rows = [lbl_ref[j:j + 1, :] for j in range(depth)]
    mx = functools.reduce(jnp.maximum, rows)
    es = [jnp.exp(r - mx) for r in rows]
    tot = functools.reduce(lambda a, b: a + b, es)
    lb = jnp.zeros_like(mx)
    for j in range(1, layer + 1):
        lb = lb + es[j] / tot

    d = HG_HEADS * HG_DK
    q = proj_ref[0, :, 0:d] * (HG_DK ** -0.5)
    f = lb + (1.0 - lb) * _sigmoid(proj_ref[0, :, d:2 * d])
    k = 1.0 - f
    expo = _dot01(dst_ref[...], jnp.log(f))
    b_cum = expo[0:c]
    q_dec = (q * jnp.exp(b_cum)).astype(BF16)
    k_dec = (k * jnp.exp(b_cum[c - 1:c] - b_cum)).astype(BF16)
    dec_last = jnp.exp(b_cum[c - 1:c])
    qs, ks = [], []
    for l in range(nlev - 1):
        w = jnp.exp(expo[(1 + l) * c:(2 + l) * c])
        qs.append((q * w).astype(BF16))
        ks.append((k * w).astype(BF16))
    e_diag = expo[nlev * c:(nlev + 1) * c]
    qs.append((q * jnp.exp(e_diag)).astype(BF16))
    ks.append((k * jnp.exp(-e_diag)).astype(BF16))
    v = proj_ref[0, :, 2 * d:3 * d].astype(BF16)
    gate = proj_ref[0, :, 3 * d:4 * d]
    gn = gn_ref[...]
    for hd in range(HG_HEADS):
        sl = slice(hd * HG_DK, (hd + 1) * HG_DK)
        a = jnp.zeros((c, c), F32)
        for l in range(nlev):
            a = a + lax.dot_general(qs[l][:, sl], ks[l][:, sl], _NT,
                                    preferred_element_type=F32) * msk_ref[l]
        st = st_ref[hd]
        o = (jnp.dot(a.astype(BF16), v[:, sl], preferred_element_type=F32)
             + lax.dot_general(q_dec[:, sl], st.astype(BF16), _NT, preferred_element_type=F32))
        st_ref[hd] = st * dec_last[:, sl] + lax.dot_general(
            v[:, sl], k_dec[:, sl], _TN, preferred_element_type=F32)
        y_ref[:, sl] = (_rms(o, gn) * _silu(gate[:, sl])).astype(BF16)
    o_ref[0] = h_ref[0] + jnp.dot(y_ref[...], wout_ref[...], preferred_element_type=F32)


def _hgrn_layer(h, proj, lb_logits, g_norm, w_out_bf16, layer):
    b, t, d = h.shape
    c = GLA_CHUNK
    dst, msk = _gla_tables(c, GLA_SUB)
    nlev = msk.shape[0]
    depth = lb_logits.shape[0]
    kern = functools.partial(_hgrn_kernel, layer=layer, depth=depth, c=c, nlev=nlev)
    return pl.pallas_call(
        kern,
        grid=(b, t // c),
        in_specs=[pl.BlockSpec((1, c, 4 * d), lambda i, j: (i, j, 0)),
                  pl.BlockSpec((1, c, d), lambda i, j: (i, j, 0)),
                  pl.BlockSpec((depth, d), lambda i, j: (0, 0)),
                  pl.BlockSpec((1, HG_DK), lambda i, j: (0, 0)),
                  pl.BlockSpec((d, d), lambda i, j: (0, 0)),
                  pl.BlockSpec(dst.shape, lambda i, j: (0, 0)),
                  pl.BlockSpec(msk.shape, lambda i, j: (0, 0, 0))],
        out_specs=pl.BlockSpec((1, c, d), lambda i, j: (i, j, 0)),
        out_shape=jax.ShapeDtypeStruct((b, t, d), F32),
        scratch_shapes=[pltpu.VMEM((HG_HEADS, HG_DK, HG_DK), F32),
                        pltpu.VMEM((c, d), BF16)],
        compiler_params=_cparams("parallel", "arbitrary"),
        name="hgrn_gla",
    )(proj, h, lb_logits, g_norm.reshape(1, HG_DK), w_out_bf16,
      jnp.asarray(dst, BF16), jnp.asarray(msk, F32))


def _log_sigmoid(x):
    return jnp.minimum(x, 0.0) - jnp.log1p(jnp.exp(-jnp.abs(x)))


def _fox_cum_kernel(fl_ref, bf_ref, ltri_ref, place_ref, o_ref, carry_ref, *, tc):
    @pl.when(pl.program_id(1) == 0)
    def _():
        carry_ref[...] = jnp.zeros_like(carry_ref)

    lf = _log_sigmoid(fl_ref[0] + bf_ref[...])
    c = _dot01(ltri_ref[...], lf) + carry_ref[...]
    carry_ref[...] = c[tc - 1:tc, :]
    hi, mid, lo = _split3(-c)
    o_ref[0] = jnp.dot(jnp.concatenate([hi, mid, lo], axis=1), place_ref[...],
                       preferred_element_type=F32).astype(BF16)


def _fox_cumsum(f_logit, b_f_pad, tc):
    b, t, _ = f_logit.shape
    ltri = np.tril(np.ones((tc, tc), np.float32))
    place = np.zeros((3 * LANES, FOX_HEADS * LANES), np.float32)
    for hd in range(FOX_HEADS):
        for part in range(3):
            place[part * LANES + hd, hd * LANES + part] = 1.0
    return pl.pallas_call(
        functools.partial(_fox_cum_kernel, tc=tc),
        grid=(b, t // tc),
        in_specs=[pl.BlockSpec((1, tc, LANES), lambda i, j: (i, j, 0)),
                  pl.BlockSpec((1, LANES), lambda i, j: (0, 0)),
                  pl.BlockSpec((tc, tc), lambda i, j: (0, 0)),
                  pl.BlockSpec(place.shape, lambda i, j: (0, 0))],
        out_specs=pl.BlockSpec((1, tc, FOX_HEADS * LANES), lambda i, j: (i, j, 0)),
        out_shape=jax.ShapeDtypeStruct((b, t, FOX_HEADS * LANES), BF16),
        scratch_shapes=[pltpu.VMEM((1, LANES), F32)],
        compiler_params=_cparams("parallel", "arbitrary"),
        name="fox_cumsum",
    )(f_logit, b_f_pad, jnp.asarray(ltri, BF16), jnp.asarray(place, BF16))


def _fox_flash_kernel(q_ref, k_ref, cf_ref, vt_ref, o_ref, qp_ref, m_ref, l_ref, acc_ref, *, tq, tk):
    qi = pl.program_id(2)
    ki = pl.program_id(3)
    lane = lax.broadcasted_iota(jnp.int32, (1, LANES), 1)

    @pl.when(ki == 0)
    def _():
        m_ref[...] = jnp.full_like(m_ref, -1e30)
        l_ref[...] = jnp.zeros_like(l_ref)
        acc_ref[...] = jnp.zeros_like(acc_ref)
        q = q_ref[0] * (FOX_DH ** -0.5)
        ones3 = jnp.broadcast_to(jnp.where(lane < 3, 1.0, 0.0).astype(BF16), (tq, LANES))
        for hh in range(2):
            qh = jnp.where((lane < FOX_DH) == (hh == 0), q, jnp.zeros_like(q))
            qp_ref[hh] = jnp.concatenate([qh, ones3], axis=1)

    def body(masked):
        k = k_ref[0]
        st = []
        for hh in range(2):
            kp = jnp.concatenate([k, cf_ref[0, :, hh * LANES:(hh + 1) * LANES]], axis=1)
            st.append(lax.dot_general(kp, qp_ref[hh], _NT, preferred_element_type=F32))
        if masked:
            key_pos = lax.broadcasted_iota(jnp.int32, (tk, tq), 0)
            qry_pos = lax.broadcasted_iota(jnp.int32, (tk, tq), 1)
            causal = key_pos <= qry_pos
        for hh in range(2):
            s = jnp.where(causal, st[hh], -1e30) if masked else st[hh]
            m_prev = m_ref[hh]
            m_new = jnp.maximum(m_prev, jnp.max(s, axis=0, keepdims=True))
            alpha = jnp.exp(m_prev - m_new)
            p = jnp.exp(s - m_new)
            l_ref[hh] = alpha * l_ref[hh] + jnp.sum(p, axis=0, keepdims=True)
            acc_ref[hh] = alpha * acc_ref[hh] + jnp.dot(
                vt_ref[0, hh * FOX_DH:(hh + 1) * FOX_DH, :], p.astype(BF16), preferred_element_type=F32)
            m_ref[hh] = m_new

    @pl.when(ki < qi)
    def _():
        body(False)

    @pl.when(ki == qi)
    def _():
        body(True)
        o_ref[0] = jnp.concatenate([acc_ref[0] / l_ref[0], acc_ref[1] / l_ref[1]], axis=0).astype(o_ref.dtype)


def _fox_flash(qk, cfeat, v_t, tq):
    b, t, _ = qk.shape
    npair = FOX_HEADS // 2
    tk = tq
    kv = lambda qi, ki: jnp.minimum(ki, qi)
    return pl.pallas_call(
        functools.partial(_fox_flash_kernel, tq=tq, tk=tk),
        grid=(b, npair, t // tq, t // tk),
        in_specs=[pl.BlockSpec((1, tq, LANES), lambda i, hp, qi, ki: (i, qi, hp)),
                  pl.BlockSpec((1, tk, LANES), lambda i, hp, qi, ki: (i, kv(qi, ki), npair + hp)),
                  pl.BlockSpec((1, tk, 2 * LANES), lambda i, hp, qi, ki: (i, kv(qi, ki), hp)),
                  pl.BlockSpec((1, LANES, tk), lambda i, hp, qi, ki: (i, hp, kv(qi, ki)))],
        out_specs=pl.BlockSpec((1, LANES, tq), lambda i, hp, qi, ki: (i, hp, qi)),
        out_shape=jax.ShapeDtypeStruct((b, D_MODEL, t), BF16),
        scratch_shapes=[pltpu.VMEM((2, tq, 2 * LANES), BF16),
                        pltpu.VMEM((2, 1, tq), F32),
                        pltpu.VMEM((2, 1, tq), F32),
                        pltpu.VMEM((2, FOX_DH, tq), F32)],
        compiler_params=_cparams("parallel", "parallel", "parallel", "arbitrary"),
        name="fox_flash",
    )(qk, qk, cfeat, v_t)


def _rms_proj_t_kernel(x_ref, g_ref, wt_ref, o_ref):
    xn = _rms(x_ref[0], g_ref[...]).astype(BF16)
    o_ref[0] = lax.dot_general(wt_ref[...], xn, _NT, preferred_element_type=F32).astype(o_ref.dtype)


def _rms_proj_t(x, g, wt_bf16, tm):
    b, t, d = x.shape
    f = wt_bf16.shape[0]
    return pl.pallas_call(
        _rms_proj_t_kernel,
        grid=(b, t // tm),
        in_specs=[pl.BlockSpec((1, tm, d), lambda i, j: (i, j, 0)),
                  pl.BlockSpec((1, d), lambda i, j: (0, 0)),
                  pl.BlockSpec((f, d), lambda i, j: (0, 0))],
        out_specs=pl.BlockSpec((1, f, tm), lambda i, j: (i, 0, j)),
        out_shape=jax.ShapeDtypeStruct((b, f, t), BF16),
        compiler_params=_cparams("parallel", "parallel"),
        name="rms_proj_t",
    )(x, g.reshape(1, d), wt_bf16)


def _mm_t_res_kernel(xt_ref, w_ref, r_ref, o_ref):
    o_ref[0] = r_ref[0] + lax.dot_general(xt_ref[0], w_ref[...], _TN, preferred_element_type=F32)


def _mm_t_res(x_t, w_bf16, res, tm):
    b, k, t = x_t.shape
    d = w_bf16.shape[1]
    return pl.pallas_call(
        _mm_t_res_kernel,
        grid=(b, t // tm),
        in_specs=[pl.BlockSpec((1, k, tm), lambda i, j: (i, 0, j)),
                  pl.BlockSpec((k, d), lambda i, j: (0, 0)),
                  pl.BlockSpec((1, tm, d), lambda i, j: (i, j, 0))],
        out_specs=pl.BlockSpec((1, tm, d), lambda i, j: (i, j, 0)),
        out_shape=jax.ShapeDtypeStruct((b, t, d), F32),
        compiler_params=_cparams("parallel", "parallel"),
        name="mm_t_res",
    )(x_t, w_bf16, res)


def _fox_layer(h, g_mix, w_in, b_f, w_out):
    b, t, d = h.shape
    n = b * t
    h2 = h.reshape(n, d)
    w_qk = w_in[:, :2 * d].astype(BF16)
    w_vt = w_in[:, 2 * d:3 * d].T.astype(BF16)
    w_f = jnp.pad(w_in[:, 3 * d:], ((0, 0), (0, LANES - FOX_HEADS))).astype(BF16)
    qk = _rms_proj(h2, g_mix, w_qk, BF16, 512, 1024).reshape(b, t, 2 * d)
    v_t = _rms_proj_t(h, g_mix, w_vt, min(512, t))
    fl = _rms_proj(h2, g_mix, w_f, F32, 512, LANES).reshape(b, t, LANES)
    b_f_pad = jnp.pad(b_f, (0, LANES - FOX_HEADS)).reshape(1, LANES)
    cfeat = _fox_cumsum(fl, b_f_pad, min(512, t))
    o_t = _fox_flash(qk, cfeat, v_t, min(FOX_TILE, t))
    return _mm_t_res(o_t, w_out.astype(BF16), h, min(512, t))


def _softplus(x):
    return jnp.maximum(x, 0.0) + jnp.log1p(jnp.exp(-jnp.abs(x)))


def _ssd_kernel(p_ref, h_ref, cw_ref, cb_ref, dtb_ref, alog_ref, dex_ref, gn_ref, wout_ref,
                ltri_ref, eexp_ref, o_ref, buf_ref, st_ref, y_ref, *, c):
    di = SSM_DI
    gn_w = SSM_GROUPS * SSM_STATE
    gw = di // SSM_GROUPS

    @pl.when(pl.program_id(1) == 0)
    def _():
        buf_ref[0:8, :] = jnp.zeros((8, buf_ref.shape[1]), F32)
        st_ref[...] = jnp.zeros_like(st_ref)

    buf_ref[8:8 + c, :] = p_ref[0, :, di:di + di + 2 * gn_w]
    conv = cb_ref[...]
    for kk in range(SSM_CONV):
        conv = conv + cw_ref[kk:kk + 1, :] * buf_ref[5 + kk:5 + kk + c, :]
    buf_ref[0:8, :] = buf_ref[c:c + 8, :]
    xbc = _silu(conv)
    xs = xbc[:, :di]
    bm = xbc[:, di:di + gn_w]
    cm = xbc[:, di + gn_w:]

    dt = _softplus(p_ref[0, :, 2 * di + 2 * gn_w:] + dtb_ref[...])
    da = dt * (-jnp.exp(alog_ref[...]))
    cum = _dot01(ltri_ref[...], da)
    cum_t = cum.T
    ex = _dot01_r(jnp.concatenate([cum, dt], axis=0), eexp_ref[...])
    cum_e = ex[:c]
    dt_e = ex[c:]
    last = cum_e[c - 1:c]
    expcum = jnp.exp(cum_e)
    declast = jnp.exp(last)
    xdt = xs * dt_e
    xdt_b = xdt.astype(BF16)
    xdec_b = (xdt * jnp.exp(last - cum_e)).astype(BF16)

    row = lax.broadcasted_iota(jnp.int32, (c, c), 0)
    col = lax.broadcasted_iota(jnp.int32, (c, c), 1)
    tril = row >= col
    lane = lax.broadcasted_iota(jnp.int32, (1, LANES), 1)
    for g in range(SSM_GROUPS):
        cg = cm[:, g * SSM_STATE:(g + 1) * SSM_STATE].astype(BF16)
        bg32 = bm[:, g * SSM_STATE:(g + 1) * SSM_STATE]
        cb = lax.dot_general(cg, bg32.astype(BF16), _NT, preferred_element_type=F32)
        st = st_ref[g]
        y_inter = (jnp.dot(cg, st.astype(BF16), preferred_element_type=F32)
                   * expcum[:, g * gw:(g + 1) * gw])
        for pr in range(2):
            base = g * gw + pr * LANES
            xpair = xdt_b[:, base:base + LANES]
            outs = []
            for hh in range(2):
                hidx = g * 4 + pr * 2 + hh
                diff = cum[:, hidx:hidx + 1] - cum_t[hidx:hidx + 1, :]
                lmat = jnp.exp(jnp.where(tril, diff, -1e30))
                outs.append(jnp.dot((cb * lmat).astype(BF16), xpair, preferred_element_type=F32))
            y_intra = jnp.where(lane < SSM_HEADDIM, outs[0], outs[1])
            y_ref[:, base:base + LANES] = (y_intra + y_inter[:, pr * LANES:(pr + 1) * LANES]
                                           + xs[:, base:base + LANES] * dex_ref[:, base:base + LANES])
        st_ref[g] = st * declast[:, g * gw:(g + 1) * gw] + jnp.dot(
            bg32.T.astype(BF16), xdec_b[:, g * gw:(g + 1) * gw], preferred_element_type=F32)

    y = y_ref[...] * _silu(p_ref[0, :, 0:di])
    for g in range(SSM_GROUPS):
        sl = slice(g * gw, (g + 1) * gw)
        y_ref[:, sl] = _rms(y[:, sl], gn_ref[:, sl])
    o_ref[0] = h_ref[0] + jnp.dot(y_ref[...].astype(BF16), wout_ref[...], preferred_element_type=F32)


def _ssd_layer(h, g_mix, w_in, conv_w, conv_b, dt_bias, a_log, d_skip, g_norm, w_out):
    b, t, d = h.shape
    n = b * t
    c = min(SSM_CHUNK, t)
    di = SSM_DI
    cdim = di + 2 * SSM_GROUPS * SSM_STATE
    pad_h = LANES - SSM_HEADS
    w_all = jnp.pad(w_in, ((0, 0), (0, pad_h))).astype(BF16)
    fdim = w_all.shape[1]
    proj = _rms_proj(h.reshape(n, d), g_mix, w_all, F32, 512, 896).reshape(b, t, fdim)
    eexp = np.zeros((LANES, di), np.float32)
    for hd in range(SSM_HEADS):
        eexp[hd, hd * SSM_HEADDIM:(hd + 1) * SSM_HEADDIM] = 1.0
    ltri = np.tril(np.ones((c, c), np.float32))
    full = lambda shape: pl.BlockSpec(shape, lambda i, j: tuple(0 for _ in shape))
    return pl.pallas_call(
        functools.partial(_ssd_kernel, c=c),
        grid=(b, t // c),
        in_specs=[pl.BlockSpec((1, c, fdim), lambda i, j: (i, j, 0)),
                  pl.BlockSpec((1, c, d), lambda i, j: (i, j, 0)),
                  full((SSM_CONV, cdim)), full((1, cdim)), full((1, LANES)), full((1, LANES)),
                  full((1, di)), full((1, di)), full((di, d)), full((c, c)), full((LANES, di))],
        out_specs=pl.BlockSpec((1, c, d), lambda i, j: (i, j, 0)),
        out_shape=jax.ShapeDtypeStruct((b, t, d), F32),
        scratch_shapes=[pltpu.VMEM((c + 8, cdim), F32),
                        pltpu.VMEM((SSM_GROUPS, SSM_STATE, di // SSM_GROUPS), F32),
                        pltpu.VMEM((c, di), F32)],
        compiler_params=_cparams("parallel", "arbitrary"),
        name="ssd_scan",
    )(proj, h, conv_w, conv_b.reshape(1, cdim),
      jnp.pad(dt_bias, (0, pad_h)).reshape(1, LANES), jnp.pad(a_log, (0, pad_h)).reshape(1, LANES),
      jnp.repeat(d_skip, SSM_HEADDIM).reshape(1, di), g_norm.reshape(1, di), w_out.astype(BF16),
      jnp.asarray(ltri, BF16), jnp.asarray(eexp, BF16))


def _stack_rows(rows):
    iota16 = lax.broadcasted_iota(jnp.int32, (PEER_TOPK, LANES), 0)
    out = jnp.zeros((PEER_TOPK, LANES), F32)
    for it, r in enumerate(rows):
        out = jnp.where(iota16 == it, r, out)
    return out


def _top16(s, exact):
    nrows = s.shape[0]
    iota = lax.broadcasted_iota(jnp.int32, (nrows, LANES), 0).astype(F32)
    rank = jnp.full((nrows, LANES), float(PEER_TOPK), F32)
    vals = []
    for it in range(PEER_TOPK):
        m = jnp.max(s, axis=0, keepdims=True)
        sel = s == m
        if exact:
            sel = iota == jnp.min(jnp.where(sel, iota, float(nrows)), axis=0, keepdims=True)
        rank = jnp.where(sel, float(it), rank)
        s = jnp.where(sel, -jnp.inf, s)
        vals.append(m)
    return _stack_rows(vals), rank


def _top16_all(scores):
    fast = [_top16(s, False) for s in scores]
    picked = jnp.zeros((1, LANES), F32)
    for _, r in fast:
        picked = picked + jnp.sum(jnp.where(r < float(PEER_TOPK), 1.0, 0.0), axis=0, keepdims=True)
    tied = jnp.max(jnp.abs(picked - float(PEER_TOPK * len(scores)))) > 0.0
    flat = lax.cond(tied,
                    lambda: tuple(x for s in scores for x in _top16(s, True)),
                    lambda: tuple(x for vr in fast for x in vr))
    return [(flat[2 * i], flat[2 * i + 1]) for i in range(len(scores))]


def _row_pair_words(x, psel):
    y = lax.bitcast_convert_type(jnp.dot(psel, x, preferred_element_type=F32), jnp.uint32)
    half = x.shape[0] // 2
    return (y[:half] >> 16) | y[half:]


def _peer_route_kernel(h_ref, g_ref, wq_ref, keys_ref, psel_ref, xn_ref, r2_ref, e2_ref, na_ref, ca_ref,
                       xs_ref, q_ref, *, mt):
    @pl.when(pl.program_id(1) == 0)
    def _():
        xn = _rms(h_ref[...], g_ref[...]).astype(BF16)
        xs_ref[...] = xn
        xn_ref[...] = xn

    q_ref[...] = jnp.dot(xs_ref[...], wq_ref[...], preferred_element_type=F32).astype(BF16)
    k1 = keys_ref[0, 0]
    k2 = keys_ref[0, 1]

    nsub = mt // LANES
    s1s, s2s = [], []
    for sb in range(nsub):
        qs = q_ref[sb * LANES:(sb + 1) * LANES, :]
        s1s.append(lax.dot_general(k1, qs[:, :PEER_HALF], _NT, preferred_element_type=F32))
        s2s.append(lax.dot_general(k2, qs[:, PEER_HALF:], _NT, preferred_element_type=F32))
    tops = _top16_all(s1s + s2s)
    iota16 = lax.broadcasted_iota(jnp.int32, (PEER_TOPK, LANES), 0).astype(F32)

    cnts = [jnp.zeros((PEER_TOPK, LANES), F32) for _ in range(nsub)]
    fronts = [tops[sb][0] + tops[nsub + sb][0][0:1] for sb in range(nsub)]
    for _ in range(PEER_TOPK):
        for sb in range(nsub):
            sv1, sv2, front, cnt = tops[sb][0], tops[nsub + sb][0], fronts[sb], cnts[sb]
            m = jnp.max(front, axis=0, keepdims=True)
            idx = jnp.min(jnp.where(front == m, iota16, float(PEER_TOPK)), axis=0, keepdims=True)
            sel = iota16 == idx
            cnt = jnp.where(sel, cnt + 1.0, cnt)
            n_sel = jnp.max(jnp.where(sel, cnt, 0.0), axis=0, keepdims=True)
            nxt = jnp.max(jnp.where(iota16 == n_sel, sv2, -jnp.inf), axis=0, keepdims=True)
            fronts[sb] = jnp.where(sel, sv1 + nxt, front)
            cnts[sb] = cnt

    for sb in range(nsub):
        sv1, rank1 = tops[sb]
        sv2, rank2 = tops[nsub + sb]
        cnt = cnts[sb]
        s2 = s2s[sb]
        v2 = [sv2[j:j + 1] for j in range(PEER_TOPK)]
        e2r = jnp.exp(sv2 - v2[0])
        pref = jnp.zeros((1, LANES), F32)
        pn = jnp.zeros((PEER_TOPK, LANES), F32)
        for j in range(PEER_TOPK):
            pref = pref + e2r[j:j + 1]
            pn = jnp.where(cnt == float(j + 1), pref, pn)
        e1 = jnp.exp(sv1 - sv1[0:1])
        z = jnp.sum(e1 * pn, axis=0, keepdims=True)
        cw = e1 * (1.0 / z)
        na = jnp.zeros((PEER_NKEYS, LANES), F32)
        ca = jnp.zeros((PEER_NKEYS, LANES), F32)
        for i in range(PEER_TOPK):
            hit = rank1 == float(i)
            na = jnp.where(hit, cnt[i:i + 1], na)
            ca = jnp.where(hit, cw[i:i + 1], ca)
        r2_ref[sb, 0] = _row_pair_words(rank2.astype(BF16), psel_ref[...])
        e2_ref[sb, 0] = _row_pair_words(jnp.exp(s2 - v2[0]).astype(BF16), psel_ref[...])
        na_ref[sb, 0] = _pair_words(na)
        ca_ref[sb, 0] = _pair_words(ca)


def _pair_words(x):
    u = lax.bitcast_convert_type(x.astype(BF16).astype(F32), jnp.uint32)
    return u | (u >> 16)


def _packed_rows(words):
    one = pltpu.bitcast(jnp.broadcast_to(words, (BF16_SUBLANES // 2, LANES)), BF16)
    return jnp.concatenate([one] * (LANES // BF16_SUBLANES), axis=0)


def _gelu(x):
    return 0.5 * x * (1.0 + lax.erf(x * (0.5 ** 0.5)))


def _peer_dense_kernel(xn_ref, u_ref, vt_ref, r2_ref, e2_ref, na_ref, ca_ref, h_ref, o_ref,
                       acc_ref, hida_ref, hidb_ref, xua_ref, xub_ref, *, mt, eb, n_eblk):
    e = pl.program_id(0)
    n_pairs = pl.num_programs(0) - 2
    e_gate = lax.rem(jnp.clip(e - 1, 0, n_pairs - 1), n_eblk)
    e_v = lax.rem(jnp.clip(e - 2, 0, n_pairs - 1), n_eblk)

    @pl.when(e == 0)
    def _():
        hida_ref[...] = jnp.zeros_like(hida_ref)
        hidb_ref[...] = jnp.zeros_like(hidb_ref)
        xub_ref[...] = jnp.zeros_like(xub_ref)

    @pl.when(e_v == 0)
    def _():
        acc_ref[...] = jnp.zeros_like(acc_ref)

    a0 = e_gate * (eb // LANES)
    zero = jnp.zeros((LANES, LANES), BF16)
    nchunk = eb // PEER_CHUNK
    dch = acc_ref.shape[0] // nchunk

    def step(xu_new, xu_old, hid_new, hid_old):
        def v_matmul(j):
            acc_ref[j * dch:(j + 1) * dch, :] += jnp.dot(vt_ref[0, j * dch:(j + 1) * dch, :], hid_old[...],
                                                         preferred_element_type=F32)

        def u_matmul(j):
            rows = slice(j * PEER_CHUNK, (j + 1) * PEER_CHUNK)
            xu_new[rows, :] = lax.dot_general(u_ref[rows, :], xn_ref[...], _NT,
                                              preferred_element_type=F32)

        def gate_tile(ablk, sb):
            gs = zero
            for hd in range(PEER_HEADS):
                r2 = pltpu.bitcast(r2_ref[sb, hd], BF16)
                e2 = pltpu.bitcast(e2_ref[sb, hd], BF16)
                n_row = _packed_rows(na_ref[sb, hd, pl.ds(a0 + ablk, 1), :])
                c_row = _packed_rows(ca_ref[sb, hd, pl.ds(a0 + ablk, 1), :])
                gs = gs + jnp.where(r2 < n_row, e2, zero) * c_row
            r0 = ablk * LANES
            act = _gelu(xu_old[r0:r0 + LANES, sb * LANES:(sb + 1) * LANES])
            hid_new[r0:r0 + LANES, sb * LANES:(sb + 1) * LANES] = act.astype(BF16) * gs

        nsub = mt // LANES
        n_ablk = PEER_CHUNK // LANES
        for j in range(nchunk):
            v_matmul(j)
            for sb in range(nsub):
                gate_tile(j * n_ablk, sb)
            u_matmul(j)
            for sb in range(nsub):
                gate_tile(j * n_ablk + 1, sb)

    @pl.when(e % 2 == 0)
    def _():
        step(xua_ref, xub_ref, hidb_ref, hida_ref)

    @pl.when(e % 2 == 1)
    def _():
        step(xub_ref, xua_ref, hida_ref, hidb_ref)

    @pl.when(jnp.logical_and(e_v == n_eblk - 1, e >= 2))
    def _():
        o_ref[...] = h_ref[...] + acc_ref[...].T


def _peer_layer(h2, g_ffn, w_q, keys, u_tab, v_tab, mt, eb):
    n, d = h2.shape
    n_eblk = u_tab.shape[0] // eb
    nsub = mt // LANES
    half = PEER_NKEYS // 2
    rshape_w = jax.ShapeDtypeStruct((n // LANES, PEER_HEADS, PEER_NKEYS, LANES), jnp.uint32)
    rshape_p = jax.ShapeDtypeStruct((n // LANES, PEER_HEADS, half, LANES), jnp.uint32)
    rspec = pl.BlockSpec((nsub, 1, PEER_NKEYS, LANES), lambda i, hd: (i, hd, 0, 0))
    pspec = pl.BlockSpec((nsub, 1, half, LANES), lambda i, hd: (i, hd, 0, 0))
    psel = np.zeros((PEER_NKEYS, PEER_NKEYS), np.float32)
    psel[np.arange(half), 2 * np.arange(half)] = 1.0
    psel[half + np.arange(half), 2 * np.arange(half) + 1] = 1.0
    xn, r2, e2, na, ca = pl.pallas_call(
        functools.partial(_peer_route_kernel, mt=mt),
        grid=(n // mt, PEER_HEADS),
        in_specs=[pl.BlockSpec((mt, d), lambda i, hd: (i, 0)),
                  pl.BlockSpec((1, d), lambda i, hd: (0, 0)),
                  pl.BlockSpec((d, 2 * PEER_HALF), lambda i, hd: (0, hd)),
                  pl.BlockSpec((1, 2, PEER_NKEYS, PEER_HALF), lambda i, hd: (hd, 0, 0, 0)),
                  pl.BlockSpec((PEER_NKEYS, PEER_NKEYS), lambda i, hd: (0, 0))],
        out_specs=[pl.BlockSpec((mt, d), lambda i, hd: (i, 0)), pspec, pspec, rspec, rspec],
        out_shape=[jax.ShapeDtypeStruct((n, d), BF16), rshape_p, rshape_p, rshape_w, rshape_w],
        scratch_shapes=[pltpu.VMEM((mt, d), BF16), pltpu.VMEM((mt, 2 * PEER_HALF), BF16)],
        compiler_params=_cparams("parallel", "arbitrary"),
        name="peer_route",
    )(h2, g_ffn.reshape(1, d), w_q.astype(BF16), keys.astype(BF16), jnp.asarray(psel, BF16))

    n_pairs = (n // mt) * n_eblk
    pair = lambda s, lag: jnp.clip(s - lag, 0, n_pairs - 1)
    dspec = pl.BlockSpec((nsub, PEER_HEADS, PEER_NKEYS, LANES), lambda s: (pair(s, 1) // n_eblk, 0, 0, 0))
    dspec_p = pl.BlockSpec((nsub, PEER_HEADS, half, LANES), lambda s: (pair(s, 1) // n_eblk, 0, 0, 0))
    v_blocks = v_tab.astype(BF16).reshape(n_eblk, eb, d).transpose(0, 2, 1)
    return pl.pallas_call(
        functools.partial(_peer_dense_kernel, mt=mt, eb=eb, n_eblk=n_eblk),
        grid=(n_pairs + 2,),
        in_specs=[pl.BlockSpec((mt, d), lambda s: (pair(s, 0) // n_eblk, 0)),
                  pl.BlockSpec((eb, d), lambda s: (pair(s, 0) % n_eblk, 0)),
                  pl.BlockSpec((1, d, eb), lambda s: (pair(s, 2) % n_eblk, 0, 0)),
                  dspec_p, dspec_p, dspec, dspec,
                  pl.BlockSpec((mt, d), lambda s: (pair(s, 2) // n_eblk, 0))],
        out_specs=pl.BlockSpec((mt, d), lambda s: (pair(s, 2) // n_eblk, 0)),
        out_shape=jax.ShapeDtypeStruct((n, d), F32),
        scratch_shapes=[pltpu.VMEM((d, mt), F32), pltpu.VMEM((eb, mt), BF16),
                        pltpu.VMEM((eb, mt), BF16), pltpu.VMEM((eb, mt), F32), pltpu.VMEM((eb, mt), F32)],
        compiler_params=_cparams("arbitrary"),
        name="peer_dense",
    )(xn, u_tab.astype(BF16), v_blocks, r2, e2, na, ca, h2)


def kernel(x, norm_mix, norm_ffn, norm_final, hgrn_lb_logits, hgrn_w_in, hgrn_gnorm, hgrn_w_out,
           fox_w_in, fox_b_f, fox_w_out, ssm_w_in, ssm_conv_w, ssm_conv_b, ssm_dt_bias, ssm_a_log,
           ssm_d, ssm_gnorm, ssm_w_out, peer_w_q, peer_keys, peer_u, peer_v):
    b, t, d = x.shape
    n = b * t
    depth = norm_mix.shape[0]
    h = x
    for i in range(depth):
        kind, j = i % 3, i // 3
        if kind == 0:
            proj = _rms_proj(h.reshape(n, d), norm_mix[i], hgrn_w_in[j].astype(BF16), F32, 512, 1024)
            h = _hgrn_layer(h, proj.reshape(b, t, 4 * d), hgrn_lb_logits, hgrn_gnorm[j],
                            hgrn_w_out[j].astype(BF16), i)
        elif kind == 1:
            h = _fox_layer(h, norm_mix[i], fox_w_in[j], fox_b_f[j], fox_w_out[j])
        else:
            h = _ssd_layer(h, norm_mix[i], ssm_w_in[j], ssm_conv_w[j], ssm_conv_b[j], ssm_dt_bias[j],
                           ssm_a_log[j], ssm_d[j], ssm_gnorm[j], ssm_w_out[j])
        h = _peer_layer(h.reshape(n, d), norm_ffn[i], peer_w_q[i], peer_keys[i], peer_u[i],
                        peer_v[i], min(512, n), 1024).reshape(b, t, d)
    return _final_norm(h.reshape(n, d), norm_final, 512).reshape(b, t, d)
```

```python
import functools
import math

import numpy as np
import jax
import jax.numpy as jnp
from jax import lax
from jax.experimental import pallas as pl
from jax.experimental.pallas import tpu as pltpu

F32 = jnp.float32
BF16 = jnp.bfloat16

NORM_EPS = 1e-6
LANES = 128
BF16_SUBLANES = 16
VMEM_LIMIT = 56 * 1024 * 1024

D_MODEL = 1024
HG_HEADS = 8
HG_DK = 128
GLA_CHUNK = 128
GLA_SUB = 16
FOX_HEADS = 16
FOX_DH = 64
FOX_TILE = 1024
SSM_DI = 2048
SSM_HEADS = 32
SSM_HEADDIM = 64
SSM_GROUPS = 8
SSM_STATE = 128
SSM_CONV = 4
SSM_CHUNK = 128
PEER_HEADS = 8
PEER_NKEYS = 128
PEER_TOPK = 16
PEER_HALF = 128
PEER_ROUTE_TOKENS = 512
PEER_DENSE_TOKENS = 1024
PEER_DENSE_EXPERTS = 512
PEER_CHUNK = 256

_NT = (((1,), (1,)), ((), ()))
_TN = (((0,), (0,)), ((), ()))


def _cparams(*sem):
    return pltpu.CompilerParams(dimension_semantics=sem, vmem_limit_bytes=VMEM_LIMIT)


def _split3(x):
    hi = x.astype(BF16)
    r = x - hi.astype(F32)
    mid = r.astype(BF16)
    lo = (r - mid.astype(F32)).astype(BF16)
    return hi, mid, lo


def _dot01(m01, x):
    hi, mid, lo = _split3(x)
    d = lambda b: jnp.dot(m01, b, preferred_element_type=F32)
    return d(hi) + d(mid) + d(lo)


def _dot01_r(x, m01):
    hi, mid, lo = _split3(x)
    d = lambda a: jnp.dot(a, m01, preferred_element_type=F32)
    return d(hi) + d(mid) + d(lo)


def _sigmoid(x):
    return 1.0 / (1.0 + jnp.exp(-x))


def _silu(x):
    return x * _sigmoid(x)


def _rms(x, g):
    ms = jnp.mean(x * x, axis=-1, keepdims=True)
    return x * lax.rsqrt(ms + NORM_EPS) * g


def _rms_proj_kernel(x_ref, g_ref, w_ref, o_ref, xn_ref):
    @pl.when(pl.program_id(1) == 0)
    def _():
        xn_ref[...] = _rms(x_ref[...], g_ref[...]).astype(BF16)

    o_ref[...] = jnp.dot(xn_ref[...], w_ref[...], preferred_element_type=F32).astype(o_ref.dtype)


def _rms_proj(x2d, g, w_bf16, out_dtype, tm, tn):
    n, d = x2d.shape
    f = w_bf16.shape[1]
    assert n % tm == 0 and f % tn == 0
    return pl.pallas_call(
        _rms_proj_kernel,
        grid=(n // tm, f // tn),
        in_specs=[pl.BlockSpec((tm, d), lambda i, j: (i, 0)),
                  pl.BlockSpec((1, d), lambda i, j: (0, 0)),
                  pl.BlockSpec((d, tn), lambda i, j: (0, j))],
        out_specs=pl.BlockSpec((tm, tn), lambda i, j: (i, j)),
        out_shape=jax.ShapeDtypeStruct((n, f), out_dtype),
        scratch_shapes=[pltpu.VMEM((tm, d), BF16)],
        compiler_params=_cparams("parallel", "arbitrary"),
        name="rms_proj",
    )(x2d, g.reshape(1, d), w_bf16)


def _mm_res_kernel(x_ref, w_ref, r_ref, o_ref):
    o_ref[...] = r_ref[...] + jnp.dot(x_ref[...], w_ref[...], preferred_element_type=F32)


def _mm_res(x2d_bf16, w_bf16, res2d, tm):
    n, k = x2d_bf16.shape
    d = w_bf16.shape[1]
    return pl.pallas_call(
        _mm_res_kernel,
        grid=(n // tm,),
        in_specs=[pl.BlockSpec((tm, k), lambda i: (i, 0)),
                  pl.BlockSpec((k, d), lambda i: (0, 0)),
                  pl.BlockSpec((tm, d), lambda i: (i, 0))],
        out_specs=pl.BlockSpec((tm, d), lambda i: (i, 0)),
        out_shape=jax.ShapeDtypeStruct((n, d), F32),
        compiler_params=_cparams("parallel"),
        name="mm_res",
    )(x2d_bf16, w_bf16, res2d)


def _final_norm_kernel(x_ref, g_ref, o_ref):
    o_ref[...] = _rms(x_ref[...], g_ref[...])


def _final_norm(x2d, g, tm):
    n, d = x2d.shape
    return pl.pallas_call(
        _final_norm_kernel,
        grid=(n // tm,),
        in_specs=[pl.BlockSpec((tm, d), lambda i: (i, 0)),
                  pl.BlockSpec((1, d), lambda i: (0, 0))],
        out_specs=pl.BlockSpec((tm, d), lambda i: (i, 0)),
        out_shape=jax.ShapeDtypeStruct((n, d), F32),
        compiler_params=_cparams("parallel"),
        name="final_norm",
    )(x2d, g.reshape(1, d))


def _gla_tables(c, sub):
    t = np.arange(c)[:, None]
    j = np.arange(c)[None, :]
    mats = [(j <= t).astype(np.float32)]
    masks = []
    bk = c // 2
    while bk >= sub:
        blk = t // bk
        odd = (blk % 2) == 1
        mats.append(((odd & (j >= blk * bk) & (j <= t))
                     | ((~odd) & (j > t) & (j < (blk + 1) * bk))).astype(np.float32))
        masks.append((odd & ((j // bk) == blk - 1)).astype(np.float32))
        bk //= 2
    m0 = (t // sub) * sub
    mats.append(((j >= m0) & (j <= t)).astype(np.float32))
    masks.append((((t // sub) == (j // sub)) & (j <= t)).astype(np.float32))
    return np.concatenate(mats, axis=0), np.stack(masks, axis=0)


def _hgrn_kernel(proj_ref, h_ref, lbl_ref, gn_ref, wout_ref, dst_ref, msk_ref, o_ref,
                 st_ref, y_ref, *, layer, depth, c, nlev):
    @pl.when(pl.program_id(1) == 0)
    def _():
        st_ref[...] = jnp.zeros_like(st_ref)

    rows = [lbl_ref[j:j + 1, :] for j in range(depth)]
    mx = functools.reduce(jnp.maximum, rows)
    es = [jnp.exp(r - mx) for r in rows]
    tot = functools.reduce(lambda a, b: a + b, es)
    lb = jnp.zeros_like(mx)
    for j in range(1, layer + 1):
        lb = lb + es[j] / tot

    d = HG_HEADS * HG_DK
    q = proj_ref[0, :, 0:d] * (HG_DK ** -0.5)
    f = lb + (1.0 - lb) * _sigmoid(proj_ref[0, :, d:2 * d])
    k = 1.0 - f
    expo = _dot01(dst_ref[...], jnp.log(f))
    b_cum = expo[0:c]
    q_dec = (q * jnp.exp(b_cum)).astype(BF16)
    k_dec = (k * jnp.exp(b_cum[c - 1:c] - b_cum)).astype(BF16)
    dec_last = jnp.exp(b_cum[c - 1:c])
    qs, ks = [], []
    for l in range(nlev - 1):
        w = jnp.exp(expo[(1 + l) * c:(2 + l) * c])
        qs.append((q * w).astype(BF16))
        ks.append((k * w).astype(BF16))
    e_diag = expo[nlev * c:(nlev + 1) * c]
    qs.append((q * jnp.exp(e_diag)).astype(BF16))
    ks.append((k * jnp.exp(-e_diag)).astype(BF16))
    v = proj_ref[0, :, 2 * d:3 * d].astype(BF16)
    gate = proj_ref[0, :, 3 * d:4 * d]
    gn = gn_ref[...]
    for hd in range(HG_HEADS):
        sl = slice(hd * HG_DK, (hd + 1) * HG_DK)
        a = jnp.zeros((c, c), F32)
        for l in range(nlev):
            a = a + lax.dot_general(qs[l][:, sl], ks[l][:, sl], _NT,
                                    preferred_element_type=F32) * msk_ref[l]
        st = st_ref[hd]
        o = (jnp.dot(a.astype(BF16), v[:, sl], preferred_element_type=F32)
             + lax.dot_general(q_dec[:, sl], st.astype(BF16), _NT, preferred_element_type=F32))
        st_ref[hd] = st * dec_last[:, sl] + lax.dot_general(
            v[:, sl], k_dec[:, sl], _TN, preferred_element_type=F32)
        y_ref[:, sl] = (_rms(o, gn) * _silu(gate[:, sl])).astype(BF16)
    o_ref[0] = h_ref[0] + jnp.dot(y_ref[...], wout_ref[...], preferred_element_type=F32)


def _hgrn_layer(h, proj, lb_logits, g_norm, w_out_bf16, layer):
    b, t, d = h.shape
    c = GLA_CHUNK
    dst, msk = _gla_tables(c, GLA_SUB)
    nlev = msk.shape[0]
    depth = lb_logits.shape[0]
    kern = functools.partial(_hgrn_kernel, layer=layer, depth=depth, c=c, nlev=nlev)
    return pl.pallas_call(
        kern,
        grid=(b, t // c),
        in_specs=[pl.BlockSpec((1, c, 4 * d), lambda i, j: (i, j, 0)),
                  pl.BlockSpec((1, c, d), lambda i, j: (i, j, 0)),
                  pl.BlockSpec((depth, d), lambda i, j: (0, 0)),
                  pl.BlockSpec((1, HG_DK), lambda i, j: (0, 0)),
                  pl.BlockSpec((d, d), lambda i, j: (0, 0)),
                  pl.BlockSpec(dst.shape, lambda i, j: (0, 0)),
                  pl.BlockSpec(msk.shape, lambda i, j: (0, 0, 0))],
        out_specs=pl.BlockSpec((1, c, d), lambda i, j: (i, j, 0)),
        out_shape=jax.ShapeDtypeStruct((b, t, d), F32),
        scratch_shapes=[pltpu.VMEM((HG_HEADS, HG_DK, HG_DK), F32),
                        pltpu.VMEM((c, d), BF16)],
        compiler_params=_cparams("parallel", "arbitrary"),
        name="hgrn_gla",
    )(proj, h, lb_logits, g_norm.reshape(1, HG_DK), w_out_bf16,
      jnp.asarray(dst, BF16), jnp.asarray(msk, F32))


def _log_sigmoid(x):
    return jnp.minimum(x, 0.0) - jnp.log1p(jnp.exp(-jnp.abs(x)))


def _fox_cum_kernel(fl_ref, bf_ref, ltri_ref, place_ref, o_ref, carry_ref, *, tc):
    @pl.when(pl.program_id(1) == 0)
    def _():
        carry_ref[...] = jnp.zeros_like(carry_ref)

    lf = _log_sigmoid(fl_ref[0] + bf_ref[...])
    c = _dot01(ltri_ref[...], lf) + carry_ref[...]
    carry_ref[...] = c[tc - 1:tc, :]
    hi, mid, lo = _split3(-c)
    o_ref[0] = jnp.dot(jnp.concatenate([hi, mid, lo], axis=1), place_ref[...],
                       preferred_element_type=F32).astype(BF16)


def _fox_cumsum(f_logit, b_f_pad, tc):
    b, t, _ = f_logit.shape
    ltri = np.tril(np.ones((tc, tc), np.float32))
    place = np.zeros((3 * LANES, FOX_HEADS * LANES), np.float32)
    for hd in range(FOX_HEADS):
        for part in range(3):
            place[part * LANES + hd, hd * LANES + part] = 1.0
    return pl.pallas_call(
        functools.partial(_fox_cum_kernel, tc=tc),
        grid=(b, t // tc),
        in_specs=[pl.BlockSpec((1, tc, LANES), lambda i, j: (i, j, 0)),
                  pl.BlockSpec((1, LANES), lambda i, j: (0, 0)),
                  pl.BlockSpec((tc, tc), lambda i, j: (0, 0)),
                  pl.BlockSpec(place.shape, lambda i, j: (0, 0))],
        out_specs=pl.BlockSpec((1, tc, FOX_HEADS * LANES), lambda i, j: (i, j, 0)),
        out_shape=jax.ShapeDtypeStruct((b, t, FOX_HEADS * LANES), BF16),
        scratch_shapes=[pltpu.VMEM((1, LANES), F32)],
        compiler_params=_cparams("parallel", "arbitrary"),
        name="fox_cumsum",
    )(f_logit, b_f_pad, jnp.asarray(ltri, BF16), jnp.asarray(place, BF16))


def _fox_flash_kernel(q_ref, k_ref, cf_ref, vt_ref, o_ref, qp_ref, m_ref, l_ref, acc_ref, *, tq, tk):
    qi = pl.program_id(2)
    ki = pl.program_id(3)
    lane = lax.broadcasted_iota(jnp.int32, (1, LANES), 1)

    @pl.when(ki == 0)
    def _():
        m_ref[...] = jnp.full_like(m_ref, -1e30)
        l_ref[...] = jnp.zeros_like(l_ref)
        acc_ref[...] = jnp.zeros_like(acc_ref)
        q = q_ref[0] * (FOX_DH ** -0.5)
        ones3 = jnp.broadcast_to(jnp.where(lane < 3, 1.0, 0.0).astype(BF16), (tq, LANES))
        for hh in range(2):
            qh = jnp.where((lane < FOX_DH) == (hh == 0), q, jnp.zeros_like(q))
            qp_ref[hh] = jnp.concatenate([qh, ones3], axis=1)

    def body(masked):
        k = k_ref[0]
        st = []
        for hh in range(2):
            kp = jnp.concatenate([k, cf_ref[0, :, hh * LANES:(hh + 1) * LANES]], axis=1)
            st.append(lax.dot_general(kp, qp_ref[hh], _NT, preferred_element_type=F32))
        if masked:
            key_pos = lax.broadcasted_iota(jnp.int32, (tk, tq), 0)
            qry_pos = lax.broadcasted_iota(jnp.int32, (tk, tq), 1)
            causal = key_pos <= qry_pos
        for hh in range(2):
            s = jnp.where(causal, st[hh], -1e30) if masked else st[hh]
            m_prev = m_ref[hh]
            m_new = jnp.maximum(m_prev, jnp.max(s, axis=0, keepdims=True))
            alpha = jnp.exp(m_prev - m_new)
            p = jnp.exp(s - m_new)
            l_ref[hh] = alpha * l_ref[hh] + jnp.sum(p, axis=0, keepdims=True)
            acc_ref[hh] = alpha * acc_ref[hh] + jnp.dot(
                vt_ref[0, hh * FOX_DH:(hh + 1) * FOX_DH, :], p.astype(BF16), preferred_element_type=F32)
            m_ref[hh] = m_new

    @pl.when(ki < qi)
    def _():
        body(False)

    @pl.when(ki == qi)
    def _():
        body(True)
        o_ref[0] = jnp.concatenate([acc_ref[0] / l_ref[0], acc_ref[1] / l_ref[1]], axis=0).astype(o_ref.dtype)


def _fox_flash(qk, cfeat, v_t, tq):
    b, t, _ = qk.shape
    npair = FOX_HEADS // 2
    tk = tq
    kv = lambda qi, ki: jnp.minimum(ki, qi)
    return pl.pallas_call(
        functools.partial(_fox_flash_kernel, tq=tq, tk=tk),
        grid=(b, npair, t // tq, t // tk),
        in_specs=[pl.BlockSpec((1, tq, LANES), lambda i, hp, qi, ki: (i, qi, hp)),
                  pl.BlockSpec((1, tk, LANES), lambda i, hp, qi, ki: (i, kv(qi, ki), npair + hp)),
                  pl.BlockSpec((1, tk, 2 * LANES), lambda i, hp, qi, ki: (i, kv(qi, ki), hp)),
                  pl.BlockSpec((1, LANES, tk), lambda i, hp, qi, ki: (i, hp, kv(qi, ki)))],
        out_specs=pl.BlockSpec((1, LANES, tq), lambda i, hp, qi, ki: (i, hp, qi)),
        out_shape=jax.ShapeDtypeStruct((b, D_MODEL, t), BF16),
        scratch_shapes=[pltpu.VMEM((2, tq, 2 * LANES), BF16),
                        pltpu.VMEM((2, 1, tq), F32),
                        pltpu.VMEM((2, 1, tq), F32),
                        pltpu.VMEM((2, FOX_DH, tq), F32)],
        compiler_params=_cparams("parallel", "parallel", "parallel", "arbitrary"),
        name="fox_flash",
    )(qk, qk, cfeat, v_t)


def _rms_proj_t_kernel(x_ref, g_ref, wt_ref, o_ref):
    xn = _rms(x_ref[0], g_ref[...]).astype(BF16)
    o_ref[0] = lax.dot_general(wt_ref[...], xn, _NT, preferred_element_type=F32).astype(o_ref.dtype)


def _rms_proj_t(x, g, wt_bf16, tm):
    b, t, d = x.shape
    f = wt_bf16.shape[0]
    return pl.pallas_call(
        _rms_proj_t_kernel,
        grid=(b, t // tm),
        in_specs=[pl.BlockSpec((1, tm, d), lambda i, j: (i, j, 0)),
                  pl.BlockSpec((1, d), lambda i, j: (0, 0)),
                  pl.BlockSpec((f, d), lambda i, j: (0, 0))],
        out_specs=pl.BlockSpec((1, f, tm), lambda i, j: (i, 0, j)),
        out_shape=jax.ShapeDtypeStruct((b, f, t), BF16),
        compiler_params=_cparams("parallel", "parallel"),
        name="rms_proj_t",
    )(x, g.reshape(1, d), wt_bf16)


def _mm_t_res_kernel(xt_ref, w_ref, r_ref, o_ref):
    o_ref[0] = r_ref[0] + lax.dot_general(xt_ref[0], w_ref[...], _TN, preferred_element_type=F32)


def _mm_t_res(x_t, w_bf16, res, tm):
    b, k, t = x_t.shape
    d = w_bf16.shape[1]
    return pl.pallas_call(
        _mm_t_res_kernel,
        grid=(b, t // tm),
        in_specs=[pl.BlockSpec((1, k, tm), lambda i, j: (i, 0, j)),
                  pl.BlockSpec((k, d), lambda i, j: (0, 0)),
                  pl.BlockSpec((1, tm, d), lambda i, j: (i, j, 0))],
        out_specs=pl.BlockSpec((1, tm, d), lambda i, j: (i, j, 0)),
        out_shape=jax.ShapeDtypeStruct((b, t, d), F32),
        compiler_params=_cparams("parallel", "parallel"),
        name="mm_t_res",
    )(x_t, w_bf16, res)


def _fox_layer(h, g_mix, w_in, b_f, w_out):
    b, t, d = h.shape
    n = b * t
    h2 = h.reshape(n, d)
    w_qk = w_in[:, :2 * d].astype(BF16)
    w_vt = w_in[:, 2 * d:3 * d].T.astype(BF16)
    w_f = jnp.pad(w_in[:, 3 * d:], ((0, 0), (0, LANES - FOX_HEADS))).astype(BF16)
    qk = _rms_proj(h2, g_mix, w_qk, BF16, 512, 1024).reshape(b, t, 2 * d)
    v_t = _rms_proj_t(h, g_mix, w_vt, min(512, t))
    fl = _rms_proj(h2, g_mix, w_f, F32, 512, LANES).reshape(b, t, LANES)
    b_f_pad = jnp.pad(b_f, (0, LANES - FOX_HEADS)).reshape(1, LANES)
    cfeat = _fox_cumsum(fl, b_f_pad, min(512, t))
    o_t = _fox_flash(qk, cfeat, v_t, min(FOX_TILE, t))
    return _mm_t_res(o_t, w_out.astype(BF16), h, min(512, t))


def _softplus(x):
    return jnp.maximum(x, 0.0) + jnp.log1p(jnp.exp(-jnp.abs(x)))


def _ssd_kernel(p_ref, h_ref, cw_ref, cb_ref, dtb_ref, alog_ref, dex_ref, gn_ref, wout_ref,
                ltri_ref, eexp_ref, o_ref, buf_ref, st_ref, y_ref, *, c):
    di = SSM_DI
    gn_w = SSM_GROUPS * SSM_STATE
    gw = di // SSM_GROUPS

    @pl.when(pl.program_id(1) == 0)
    def _():
        buf_ref[0:8, :] = jnp.zeros((8, buf_ref.shape[1]), F32)
        st_ref[...] = jnp.zeros_like(st_ref)

    buf_ref[8:8 + c, :] = p_ref[0, :, di:di + di + 2 * gn_w]
    conv = cb_ref[...]
    for kk in range(SSM_CONV):
        conv = conv + cw_ref[kk:kk + 1, :] * buf_ref[5 + kk:5 + kk + c, :]
    buf_ref[0:8, :] = buf_ref[c:c + 8, :]
    xbc = _silu(conv)
    xs = xbc[:, :di]
    bm = xbc[:, di:di + gn_w]
    cm = xbc[:, di + gn_w:]

    dt = _softplus(p_ref[0, :, 2 * di + 2 * gn_w:] + dtb_ref[...])
    da = dt * (-jnp.exp(alog_ref[...]))
    cum = _dot01(ltri_ref[...], da)
    cum_t = cum.T
    ex = _dot01_r(jnp.concatenate([cum, dt], axis=0), eexp_ref[...])
    cum_e = ex[:c]
    dt_e = ex[c:]
    last = cum_e[c - 1:c]
    expcum = jnp.exp(cum_e)
    declast = jnp.exp(last)
    xdt = xs * dt_e
    xdt_b = xdt.astype(BF16)
    xdec_b = (xdt * jnp.exp(last - cum_e)).astype(BF16)

    row = lax.broadcasted_iota(jnp.int32, (c, c), 0)
    col = lax.broadcasted_iota(jnp.int32, (c, c), 1)
    tril = row >= col
    lane = lax.broadcasted_iota(jnp.int32, (1, LANES), 1)
    for g in range(SSM_GROUPS):
        cg = cm[:, g * SSM_STATE:(g + 1) * SSM_STATE].astype(BF16)
        bg32 = bm[:, g * SSM_STATE:(g + 1) * SSM_STATE]
        cb = lax.dot_general(cg, bg32.astype(BF16), _NT, preferred_element_type=F32)
        st = st_ref[g]
        y_inter = (jnp.dot(cg, st.astype(BF16), preferred_element_type=F32)
                   * expcum[:, g * gw:(g + 1) * gw])
        for pr in range(2):
            base = g * gw + pr * LANES
            xpair = xdt_b[:, base:base + LANES]
            outs = []
            for hh in range(2):
                hidx = g * 4 + pr * 2 + hh
                diff = cum[:, hidx:hidx + 1] - cum_t[hidx:hidx + 1, :]
                lmat = jnp.exp(jnp.where(tril, diff, -1e30))
                outs.append(jnp.dot((cb * lmat).astype(BF16), xpair, preferred_element_type=F32))
            y_intra = jnp.where(lane < SSM_HEADDIM, outs[0], outs[1])
            y_ref[:, base:base + LANES] = (y_intra + y_inter[:, pr * LANES:(pr + 1) * LANES]
                                           + xs[:, base:base + LANES] * dex_ref[:, base:base + LANES])
        st_ref[g] = st * declast[:, g * gw:(g + 1) * gw] + jnp.dot(
            bg32.T.astype(BF16), xdec_b[:, g * gw:(g + 1) * gw], preferred_element_type=F32)

    y = y_ref[...] * _silu(p_ref[0, :, 0:di])
    for g in range(SSM_GROUPS):
        sl = slice(g * gw, (g + 1) * gw)
        y_ref[:, sl] = _rms(y[:, sl], gn_ref[:, sl])
    o_ref[0] = h_ref[0] + jnp.dot(y_ref[...].astype(BF16), wout_ref[...], preferred_element_type=F32)


def _ssd_layer(h, g_mix, w_in, conv_w, conv_b, dt_bias, a_log, d_skip, g_norm, w_out):
    b, t, d = h.shape
    n = b * t
    c = min(SSM_CHUNK, t)
    di = SSM_DI
    cdim = di + 2 * SSM_GROUPS * SSM_STATE
    pad_h = LANES - SSM_HEADS
    w_all = jnp.pad(w_in, ((0, 0), (0, pad_h))).astype(BF16)
    fdim = w_all.shape[1]
    proj = _rms_proj(h.reshape(n, d), g_mix, w_all, F32, 512, 896).reshape(b, t, fdim)
    eexp = np.zeros((LANES, di), np.float32)
    for hd in range(SSM_HEADS):
        eexp[hd, hd * SSM_HEADDIM:(hd + 1) * SSM_HEADDIM] = 1.0
    ltri = np.tril(np.ones((c, c), np.float32))
    full = lambda shape: pl.BlockSpec(shape, lambda i, j: tuple(0 for _ in shape))
    return pl.pallas_call(
        functools.partial(_ssd_kernel, c=c),
        grid=(b, t // c),
        in_specs=[pl.BlockSpec((1, c, fdim), lambda i, j: (i, j, 0)),
                  pl.BlockSpec((1, c, d), lambda i, j: (i, j, 0)),
                  full((SSM_CONV, cdim)), full((1, cdim)), full((1, LANES)), full((1, LANES)),
                  full((1, di)), full((1, di)), full((di, d)), full((c, c)), full((LANES, di))],
        out_specs=pl.BlockSpec((1, c, d), lambda i, j: (i, j, 0)),
        out_shape=jax.ShapeDtypeStruct((b, t, d), F32),
        scratch_shapes=[pltpu.VMEM((c + 8, cdim), F32),
                        pltpu.VMEM((SSM_GROUPS, SSM_STATE, di // SSM_GROUPS), F32),
                        pltpu.VMEM((c, di), F32)],
        compiler_params=_cparams("parallel", "arbitrary"),
        name="ssd_scan",
    )(proj, h, conv_w, conv_b.reshape(1, cdim),
      jnp.pad(dt_bias, (0, pad_h)).reshape(1, LANES), jnp.pad(a_log, (0, pad_h)).reshape(1, LANES),
      jnp.repeat(d_skip, SSM_HEADDIM).reshape(1, di), g_norm.reshape(1, di), w_out.astype(BF16),
      jnp.asarray(ltri, BF16), jnp.asarray(eexp, BF16))


def _stack_rows(rows):
    iota16 = lax.broadcasted_iota(jnp.int32, (PEER_TOPK, LANES), 0)
    out = jnp.zeros((PEER_TOPK, LANES), F32)
    for it, r in enumerate(rows):
        out = jnp.where(iota16 == it, r, out)
    return out


def _top16(s, exact):
    nrows = s.shape[0]
    iota = lax.broadcasted_iota(jnp.int32, (nrows, LANES), 0).astype(F32)
    rank = jnp.full((nrows, LANES), float(PEER_TOPK), F32)
    vals = []
    for it in range(PEER_TOPK):
        m = jnp.max(s, axis=0, keepdims=True)
        sel = s == m
        if exact:
            sel = iota == jnp.min(jnp.where(sel, iota, float(nrows)), axis=0, keepdims=True)
        rank = jnp.where(sel, float(it), rank)
        s = jnp.where(sel, -jnp.inf, s)
        vals.append(m)
    return _stack_rows(vals), rank


def _top16_all(scores):
    fast = [_top16(s, False) for s in scores]
    picked = jnp.zeros((1, LANES), F32)
    for _, r in fast:
        picked = picked + jnp.sum(jnp.where(r < float(PEER_TOPK), 1.0, 0.0), axis=0, keepdims=True)
    tied = jnp.max(jnp.abs(picked - float(PEER_TOPK * len(scores)))) > 0.0
    flat = lax.cond(tied,
                    lambda: tuple(x for s in scores for x in _top16(s, True)),
                    lambda: tuple(x for vr in fast for x in vr))
    return [(flat[2 * i], flat[2 * i + 1]) for i in range(len(scores))]


def _row_pair_words(x, psel):
    y = lax.bitcast_convert_type(jnp.dot(psel, x, preferred_element_type=F32), jnp.uint32)
    half = x.shape[0] // 2
    return (y[:half] >> 16) | y[half:]


def _peer_route_kernel(h_ref, g_ref, wq_ref, keys_ref, psel_ref, xn_ref, r2_ref, e2_ref, na_ref, ca_ref,
                       xs_ref, q_ref, *, mt):
    @pl.when(pl.program_id(1) == 0)
    def _():
        xn = _rms(h_ref[...], g_ref[...]).astype(BF16)
        xs_ref[...] = xn
        xn_ref[...] = xn

    q_ref[...] = jnp.dot(xs_ref[...], wq_ref[...], preferred_element_type=F32).astype(BF16)
    k1 = keys_ref[0, 0]
    k2 = keys_ref[0, 1]

    nsub = mt // LANES
    s1s, s2s = [], []
    for sb in range(nsub):
        qs = q_ref[sb * LANES:(sb + 1) * LANES, :]
        s1s.append(lax.dot_general(k1, qs[:, :PEER_HALF], _NT, preferred_element_type=F32))
        s2s.append(lax.dot_general(k2, qs[:, PEER_HALF:], _NT, preferred_element_type=F32))
    tops = _top16_all(s1s + s2s)
    iota16 = lax.broadcasted_iota(jnp.int32, (PEER_TOPK, LANES), 0).astype(F32)

    cnts = [jnp.zeros((PEER_TOPK, LANES), F32) for _ in range(nsub)]
    fronts = [tops[sb][0] + tops[nsub + sb][0][0:1] for sb in range(nsub)]
    for _ in range(PEER_TOPK):
        for sb in range(nsub):
            sv1, sv2, front, cnt = tops[sb][0], tops[nsub + sb][0], fronts[sb], cnts[sb]
            m = jnp.max(front, axis=0, keepdims=True)
            idx = jnp.min(jnp.where(front == m, iota16, float(PEER_TOPK)), axis=0, keepdims=True)
            sel = iota16 == idx
            cnt = jnp.where(sel, cnt + 1.0, cnt)
            n_sel = jnp.max(jnp.where(sel, cnt, 0.0), axis=0, keepdims=True)
            nxt = jnp.max(jnp.where(iota16 == n_sel, sv2, -jnp.inf), axis=0, keepdims=True)
            fronts[sb] = jnp.where(sel, sv1 + nxt, front)
            cnts[sb] = cnt

    for sb in range(nsub):
        sv1, rank1 = tops[sb]
        sv2, rank2 = tops[nsub + sb]
        cnt = cnts[sb]
        s2 = s2s[sb]
        v2 = [sv2[j:j + 1] for j in range(PEER_TOPK)]
        e2r = jnp.exp(sv2 - v2[0])
        pref = jnp.zeros((1, LANES), F32)
        pn = jnp.zeros((PEER_TOPK, LANES), F32)
        for j in range(PEER_TOPK):
            pref = pref + e2r[j:j + 1]
            pn = jnp.where(cnt == float(j + 1), pref, pn)
        e1 = jnp.exp(sv1 - sv1[0:1])
        z = jnp.sum(e1 * pn, axis=0, keepdims=True)
        cw = e1 * (1.0 / z)
        na = jnp.zeros((PEER_NKEYS, LANES), F32)
        ca = jnp.zeros((PEER_NKEYS, LANES), F32)
        for i in range(PEER_TOPK):
            hit = rank1 == float(i)
            na = jnp.where(hit, cnt[i:i + 1], na)
            ca = jnp.where(hit, cw[i:i + 1], ca)
        r2_ref[sb, 0] = _row_pair_words(rank2.astype(BF16), psel_ref[...])
        e2_ref[sb, 0] = _row_pair_words(jnp.exp(s2 - v2[0]).astype(BF16), psel_ref[...])
        na_ref[sb, 0] = _pair_words(na)
        ca_ref[sb, 0] = _pair_words(ca)


def _pair_words(x):
    u = lax.bitcast_convert_type(x.astype(BF16).astype(F32), jnp.uint32)
    return u | (u >> 16)


def _packed_rows(words):
    one = pltpu.bitcast(jnp.broadcast_to(words, (BF16_SUBLANES // 2, LANES)), BF16)
    return jnp.concatenate([one] * (LANES // BF16_SUBLANES), axis=0)


def _gelu(x):
    return 0.5 * x * (1.0 + lax.erf(x * (0.5 ** 0.5)))


def _peer_dense_kernel(xn_ref, u_ref, vt_ref, r2_ref, e2_ref, na_ref, ca_ref, h_ref, o_ref,
                       acc_ref, hida_ref, hidb_ref, xua_ref, xub_ref, *, mt, eb, n_eblk):
    e = pl.program_id(0)
    n_pairs = pl.num_programs(0) - 2
    e_gate = lax.rem(jnp.clip(e - 1, 0, n_pairs - 1), n_eblk)
    e_v = lax.rem(jnp.clip(e - 2, 0, n_pairs - 1), n_eblk)

    @pl.when(e == 0)
    def _():
        hida_ref[...] = jnp.zeros_like(hida_ref)
        hidb_ref[...] = jnp.zeros_like(hidb_ref)
        xub_ref[...] = jnp.zeros_like(xub_ref)

    @pl.when(e_v == 0)
    def _():
        acc_ref[...] = jnp.zeros_like(acc_ref)

    zero = jnp.zeros((LANES, LANES), BF16)
    nchunk = eb // PEER_CHUNK
    dch = acc_ref.shape[0] // nchunk

    def step(xu_new, xu_old, hid_new, hid_old):
        def v_matmul(j):
            acc_ref[j * dch:(j + 1) * dch, :] += jnp.dot(vt_ref[0, j * dch:(j + 1) * dch, :], hid_old[...],
                                                         preferred_element_type=F32)

        def u_matmul(j):
            rows = slice(j * PEER_CHUNK, (j + 1) * PEER_CHUNK)
            xu_new[rows, :] = lax.dot_general(u_ref[rows, :], xn_ref[...], _NT,
                                              preferred_element_type=F32)

        def gate_tile(ablk, sb):
            gs = zero
            for hd in range(PEER_HEADS):
                r2 = pltpu.bitcast(r2_ref[sb, hd], BF16)
                e2 = pltpu.bitcast(e2_ref[sb, hd], BF16)
                n_row = _packed_rows(na_ref[sb, hd, 0, ablk:ablk + 1, :])
                c_row = _packed_rows(ca_ref[sb, hd, 0, ablk:ablk + 1, :])
                gs = gs + jnp.where(r2 < n_row, e2, zero) * c_row
            r0 = ablk * LANES
            act = _gelu(xu_old[r0:r0 + LANES, sb * LANES:(sb + 1) * LANES])
            hid_new[r0:r0 + LANES, sb * LANES:(sb + 1) * LANES] = act.astype(BF16) * gs

        nsub = mt // LANES
        n_ablk = PEER_CHUNK // LANES
        for j in range(nchunk):
            v_matmul(j)
            for sb in range(nsub):
                gate_tile(j * n_ablk, sb)
            u_matmul(j)
            for sb in range(nsub):
                gate_tile(j * n_ablk + 1, sb)

    @pl.when(e % 2 == 0)
    def _():
        step(xua_ref, xub_ref, hidb_ref, hida_ref)

    @pl.when(e % 2 == 1)
    def _():
        step(xub_ref, xua_ref, hida_ref, hidb_ref)

    @pl.when(jnp.logical_and(e_v == n_eblk - 1, e >= 2))
    def _():
        o_ref[...] = h_ref[...] + acc_ref[...].T


def _peer_layer(h2, g_ffn, w_q, keys, u_tab, v_tab, mt, mt_dense, eb):
    n, d = h2.shape
    n_eblk = u_tab.shape[0] // eb
    nsub = mt // LANES
    half = PEER_NKEYS // 2
    rshape_w = jax.ShapeDtypeStruct((n // LANES, PEER_HEADS, PEER_NKEYS, LANES), jnp.uint32)
    rshape_p = jax.ShapeDtypeStruct((n // LANES, PEER_HEADS, half, LANES), jnp.uint32)
    rspec = pl.BlockSpec((nsub, 1, PEER_NKEYS, LANES), lambda i, hd: (i, hd, 0, 0))
    pspec = pl.BlockSpec((nsub, 1, half, LANES), lambda i, hd: (i, hd, 0, 0))
    psel = np.zeros((PEER_NKEYS, PEER_NKEYS), np.float32)
    psel[np.arange(half), 2 * np.arange(half)] = 1.0
    psel[half + np.arange(half), 2 * np.arange(half) + 1] = 1.0
    xn, r2, e2, na, ca = pl.pallas_call(
        functools.partial(_peer_route_kernel, mt=mt),
        grid=(n // mt, PEER_HEADS),
        in_specs=[pl.BlockSpec((mt, d), lambda i, hd: (i, 0)),
                  pl.BlockSpec((1, d), lambda i, hd: (0, 0)),
                  pl.BlockSpec((d, 2 * PEER_HALF), lambda i, hd: (0, hd)),
                  pl.BlockSpec((1, 2, PEER_NKEYS, PEER_HALF), lambda i, hd: (hd, 0, 0, 0)),
                  pl.BlockSpec((PEER_NKEYS, PEER_NKEYS), lambda i, hd: (0, 0))],
        out_specs=[pl.BlockSpec((mt, d), lambda i, hd: (i, 0)), pspec, pspec, rspec, rspec],
        out_shape=[jax.ShapeDtypeStruct((n, d), BF16), rshape_p, rshape_p, rshape_w, rshape_w],
        scratch_shapes=[pltpu.VMEM((mt, d), BF16), pltpu.VMEM((mt, 2 * PEER_HALF), BF16)],
        compiler_params=_cparams("parallel", "arbitrary"),
        name="peer_route",
    )(h2, g_ffn.reshape(1, d), w_q.astype(BF16), keys.astype(BF16), jnp.asarray(psel, BF16))

    mtd = mt_dense
    nsub_d = mtd // LANES
    n_pairs = (n // mtd) * n_eblk
    pair = lambda s, lag: jnp.clip(s - lag, 0, n_pairs - 1)
    n_ablk = eb // LANES
    blocked = lambda x: x.reshape(n // LANES, PEER_HEADS, n_eblk, n_ablk, LANES)
    dspec = pl.BlockSpec((nsub_d, PEER_HEADS, 1, n_ablk, LANES),
                         lambda s: (pair(s, 1) // n_eblk, 0, pair(s, 1) % n_eblk, 0, 0))
    dspec_p = pl.BlockSpec((nsub_d, PEER_HEADS, half, LANES), lambda s: (pair(s, 1) // n_eblk, 0, 0, 0))
    v_blocks = v_tab.astype(BF16).reshape(n_eblk, eb, d).transpose(0, 2, 1)
    return pl.pallas_call(
        functools.partial(_peer_dense_kernel, mt=mtd, eb=eb, n_eblk=n_eblk),
        grid=(n_pairs + 2,),
        in_specs=[pl.BlockSpec((mtd, d), lambda s: (pair(s, 0) // n_eblk, 0)),
                  pl.BlockSpec((eb, d), lambda s: (pair(s, 0) % n_eblk, 0)),
                  pl.BlockSpec((1, d, eb), lambda s: (pair(s, 2) % n_eblk, 0, 0)),
                  dspec_p, dspec_p, dspec, dspec,
                  pl.BlockSpec((mtd, d), lambda s: (pair(s, 2) // n_eblk, 0))],
        out_specs=pl.BlockSpec((mtd, d), lambda s: (pair(s, 2) // n_eblk, 0)),
        out_shape=jax.ShapeDtypeStruct((n, d), F32),
        scratch_shapes=[pltpu.VMEM((d, mtd), F32), pltpu.VMEM((eb, mtd), BF16),
                        pltpu.VMEM((eb, mtd), BF16), pltpu.VMEM((eb, mtd), F32), pltpu.VMEM((eb, mtd), F32)],
        compiler_params=_cparams("arbitrary"),
        name="peer_dense",
    )(xn, u_tab.astype(BF16), v_blocks, r2, e2, blocked(na), blocked(ca), h2)


def kernel(x, norm_mix, norm_ffn, norm_final, hgrn_lb_logits, hgrn_w_in, hgrn_gnorm, hgrn_w_out,
           fox_w_in, fox_b_f, fox_w_out, ssm_w_in, ssm_conv_w, ssm_conv_b, ssm_dt_bias, ssm_a_log,
           ssm_d, ssm_gnorm, ssm_w_out, peer_w_q, peer_keys, peer_u, peer_v):
    b, t, d = x.shape
    n = b * t
    depth = norm_mix.shape[0]
    h = x
    for i in range(depth):
        kind, j = i % 3, i // 3
        if kind == 0:
            proj = _rms_proj(h.reshape(n, d), norm_mix[i], hgrn_w_in[j].astype(BF16), F32, 512, 1024)
            h = _hgrn_layer(h, proj.reshape(b, t, 4 * d), hgrn_lb_logits, hgrn_gnorm[j],
                            hgrn_w_out[j].astype(BF16), i)
        elif kind == 1:
            h = _fox_layer(h, norm_mix[i], fox_w_in[j], fox_b_f[j], fox_w_out[j])
        else:
            h = _ssd_layer(h, norm_mix[i], ssm_w_in[j], ssm_conv_w[j], ssm_conv_b[j], ssm_dt_bias[j],
                           ssm_a_log[j], ssm_d[j], ssm_gnorm[j], ssm_w_out[j])
        h = _peer_layer(h.reshape(n, d), norm_ffn[i], peer_w_q[i], peer_keys[i], peer_u[i],
                        peer_v[i], min(PEER_ROUTE_TOKENS, n), min(PEER_DENSE_TOKENS, n), PEER_DENSE_EXPERTS).reshape(b, t, d)
    return _final_norm(h.reshape(n, d), norm_final, 512).reshape(b, t, d)
```

```python
import functools
import math

import numpy as np
import jax
import jax.numpy as jnp
from jax import lax
from jax.experimental import pallas as pl
from jax.experimental.pallas import tpu as pltpu

F32 = jnp.float32
BF16 = jnp.bfloat16

NORM_EPS = 1e-6
LANES = 128
BF16_SUBLANES = 16
VMEM_LIMIT = 56 * 1024 * 1024
PROJ_TOKENS = 1024

D_MODEL = 1024
HG_HEADS = 8
HG_DK = 128
GLA_CHUNK = 128
GLA_SUB = 16
FOX_HEADS = 16
FOX_DH = 64
FOX_TQ = 1024
FOX_TK = 1024
SSM_DI = 2048
SSM_HEADS = 32
SSM_HEADDIM = 64
SSM_GROUPS = 8
SSM_STATE = 128
SSM_CONV = 4
SSM_CHUNK = 128
PEER_HEADS = 8
PEER_NKEYS = 128
PEER_TOPK = 16
PEER_HALF = 128
PEER_ROUTE_TOKENS = 512
PEER_DENSE_TOKENS = 1024
PEER_DENSE_EXPERTS = 512
PEER_CHUNK = 256

_NT = (((1,), (1,)), ((), ()))
_TN = (((0,), (0,)), ((), ()))


def _cparams(*sem):
    return pltpu.CompilerParams(dimension_semantics=sem, vmem_limit_bytes=VMEM_LIMIT)


def _split3(x):
    hi = x.astype(BF16)
    r = x - hi.astype(F32)
    mid = r.astype(BF16)
    lo = (r - mid.astype(F32)).astype(BF16)
    return hi, mid, lo


def _dot01(m01, x):
    hi, mid, lo = _split3(x)
    d = lambda b: jnp.dot(m01, b, preferred_element_type=F32)
    return d(hi) + d(mid) + d(lo)


def _dot01_r(x, m01):
    hi, mid, lo = _split3(x)
    d = lambda a: jnp.dot(a, m01, preferred_element_type=F32)
    return d(hi) + d(mid) + d(lo)


def _sigmoid(x):
    return 1.0 / (1.0 + jnp.exp(-x))


def _silu(x):
    return x * _sigmoid(x)


def _rms(x, g):
    ms = jnp.mean(x * x, axis=-1, keepdims=True)
    return x * lax.rsqrt(ms + NORM_EPS) * g


def _rms_proj_kernel(x_ref, g_ref, w_ref, o_ref, xn_ref):
    @pl.when(pl.program_id(1) == 0)
    def _():
        xn_ref[...] = _rms(x_ref[...], g_ref[...]).astype(BF16)

    o_ref[...] = jnp.dot(xn_ref[...], w_ref[...], preferred_element_type=F32).astype(o_ref.dtype)


def _rms_proj(x2d, g, w_bf16, out_dtype, tm, tn):
    n, d = x2d.shape
    f = w_bf16.shape[1]
    assert n % tm == 0 and f % tn == 0
    return pl.pallas_call(
        _rms_proj_kernel,
        grid=(n // tm, f // tn),
        in_specs=[pl.BlockSpec((tm, d), lambda i, j: (i, 0)),
                  pl.BlockSpec((1, d), lambda i, j: (0, 0)),
                  pl.BlockSpec((d, tn), lambda i, j: (0, j))],
        out_specs=pl.BlockSpec((tm, tn), lambda i, j: (i, j)),
        out_shape=jax.ShapeDtypeStruct((n, f), out_dtype),
        scratch_shapes=[pltpu.VMEM((tm, d), BF16)],
        compiler_params=_cparams("parallel", "arbitrary"),
        name="rms_proj",
    )(x2d, g.reshape(1, d), w_bf16)


def _mm_res_kernel(x_ref, w_ref, r_ref, o_ref):
    o_ref[...] = r_ref[...] + jnp.dot(x_ref[...], w_ref[...], preferred_element_type=F32)


def _mm_res(x2d_bf16, w_bf16, res2d, tm):
    n, k = x2d_bf16.shape
    d = w_bf16.shape[1]
    return pl.pallas_call(
        _mm_res_kernel,
        grid=(n // tm,),
        in_specs=[pl.BlockSpec((tm, k), lambda i: (i, 0)),
                  pl.BlockSpec((k, d), lambda i: (0, 0)),
                  pl.BlockSpec((tm, d), lambda i: (i, 0))],
        out_specs=pl.BlockSpec((tm, d), lambda i: (i, 0)),
        out_shape=jax.ShapeDtypeStruct((n, d), F32),
        compiler_params=_cparams("parallel"),
        name="mm_res",
    )(x2d_bf16, w_bf16, res2d)


def _final_norm_kernel(x_ref, g_ref, o_ref):
    o_ref[...] = _rms(x_ref[...], g_ref[...])


def _final_norm(x2d, g, tm):
    n, d = x2d.shape
    return pl.pallas_call(
        _final_norm_kernel,
        grid=(n // tm,),
        in_specs=[pl.BlockSpec((tm, d), lambda i: (i, 0)),
                  pl.BlockSpec((1, d), lambda i: (0, 0))],
        out_specs=pl.BlockSpec((tm, d), lambda i: (i, 0)),
        out_shape=jax.ShapeDtypeStruct((n, d), F32),
        compiler_params=_cparams("parallel"),
        name="final_norm",
    )(x2d, g.reshape(1, d))


def _gla_tables(c, sub):
    t = np.arange(c)[:, None]
    j = np.arange(c)[None, :]
    mats = [(j <= t).astype(np.float32)]
    masks = []
    bk = c // 2
    while bk >= sub:
        blk = t // bk
        odd = (blk % 2) == 1
        mats.append(((odd & (j >= blk * bk) & (j <= t))
                     | ((~odd) & (j > t) & (j < (blk + 1) * bk))).astype(np.float32))
        masks.append((odd & ((j // bk) == blk - 1)).astype(np.float32))
        bk //= 2
    m0 = (t // sub) * sub
    mats.append(((j >= m0) & (j <= t)).astype(np.float32))
    masks.append((((t // sub) == (j // sub)) & (j <= t)).astype(np.float32))
    return np.concatenate(mats, axis=0), np.stack(masks, axis=0)


def _hgrn_kernel(proj_ref, h_ref, lbl_ref, gn_ref, wout_ref, dst_ref, msk_ref, o_ref,
                 st_ref, y_ref, *, layer, depth, c, nlev):
    @pl.when(pl.program_id(1) == 0)
    def _():
        st_ref[...] = jnp.zeros_like(st_ref)

    rows = [lbl_ref[j:j + 1, :] for j in range(depth)]
    mx = functools.reduce(jnp.maximum, rows)
    es = [jnp.exp(r - mx) for r in rows]
    tot = functools.reduce(lambda a, b: a + b, es)
    lb = jnp.zeros_like(mx)
    for j in range(1, layer + 1):
        lb = lb + es[j] / tot

    d = HG_HEADS * HG_DK
    q = proj_ref[0, :, 0:d] * (HG_DK ** -0.5)
    f = lb + (1.0 - lb) * _sigmoid(proj_ref[0, :, d:2 * d])
    k = 1.0 - f
    expo = _dot01(dst_ref[...], jnp.log(f))
    b_cum = expo[0:c]
    q_dec = (q * jnp.exp(b_cum)).astype(BF16)
    k_dec = (k * jnp.exp(b_cum[c - 1:c] - b_cum)).astype(BF16)
    dec_last = jnp.exp(b_cum[c - 1:c])
    qs, ks = [], []
    for l in range(nlev - 1):
        w = jnp.exp(expo[(1 + l) * c:(2 + l) * c])
        qs.append((q * w).astype(BF16))
        ks.append((k * w).astype(BF16))
    e_diag = expo[nlev * c:(nlev + 1) * c]
    qs.append((q * jnp.exp(e_diag)).astype(BF16))
    ks.append((k * jnp.exp(-e_diag)).astype(BF16))
    v = proj_ref[0, :, 2 * d:3 * d].astype(BF16)
    gate = proj_ref[0, :, 3 * d:4 * d]
    gn = gn_ref[...]
    for hd in range(HG_HEADS):
        sl = slice(hd * HG_DK, (hd + 1) * HG_DK)
        a = jnp.zeros((c, c), F32)
        for l in range(nlev):
            a = a + lax.dot_general(qs[l][:, sl], ks[l][:, sl], _NT,
                                    preferred_element_type=F32) * msk_ref[l]
        st = st_ref[hd]
        o = (jnp.dot(a.astype(BF16), v[:, sl], preferred_element_type=F32)
             + lax.dot_general(q_dec[:, sl], st.astype(BF16), _NT, preferred_element_type=F32))
        st_ref[hd] = st * dec_last[:, sl] + lax.dot_general(
            v[:, sl], k_dec[:, sl], _TN, preferred_element_type=F32)
        y_ref[:, sl] = (_rms(o, gn) * _silu(gate[:, sl])).astype(BF16)
    o_ref[0] = h_ref[0] + jnp.dot(y_ref[...], wout_ref[...], preferred_element_type=F32)


def _hgrn_layer(h, proj, lb_logits, g_norm, w_out_bf16, layer):
    b, t, d = h.shape
    c = GLA_CHUNK
    dst, msk = _gla_tables(c, GLA_SUB)
    nlev = msk.shape[0]
    depth = lb_logits.shape[0]
    kern = functools.partial(_hgrn_kernel, layer=layer, depth=depth, c=c, nlev=nlev)
    return pl.pallas_call(
        kern,
        grid=(b, t // c),
        in_specs=[pl.BlockSpec((1, c, 4 * d), lambda i, j: (i, j, 0)),
                  pl.BlockSpec((1, c, d), lambda i, j: (i, j, 0)),
                  pl.BlockSpec((depth, d), lambda i, j: (0, 0)),
                  pl.BlockSpec((1, HG_DK), lambda i, j: (0, 0)),
                  pl.BlockSpec((d, d), lambda i, j: (0, 0)),
                  pl.BlockSpec(dst.shape, lambda i, j: (0, 0)),
                  pl.BlockSpec(msk.shape, lambda i, j: (0, 0, 0))],
        out_specs=pl.BlockSpec((1, c, d), lambda i, j: (i, j, 0)),
        out_shape=jax.ShapeDtypeStruct((b, t, d), F32),
        scratch_shapes=[pltpu.VMEM((HG_HEADS, HG_DK, HG_DK), F32),
                        pltpu.VMEM((c, d), BF16)],
        compiler_params=_cparams("parallel", "arbitrary"),
        name="hgrn_gla",
    )(proj, h, lb_logits, g_norm.reshape(1, HG_DK), w_out_bf16,
      jnp.asarray(dst, BF16), jnp.asarray(msk, F32))


def _log_sigmoid(x):
    return jnp.minimum(x, 0.0) - jnp.log1p(jnp.exp(-jnp.abs(x)))


def _fox_cum_kernel(fl_ref, bf_ref, ltri_ref, place_ref, o_ref, carry_ref, *, tc):
    @pl.when(pl.program_id(1) == 0)
    def _():
        carry_ref[...] = jnp.zeros_like(carry_ref)

    lf = _log_sigmoid(fl_ref[0] + bf_ref[...])
    c = _dot01(ltri_ref[...], lf) + carry_ref[...]
    carry_ref[...] = c[tc - 1:tc, :]
    hi, mid, lo = _split3(-c)
    o_ref[0] = jnp.dot(jnp.concatenate([hi, mid, lo], axis=1), place_ref[...],
                       preferred_element_type=F32).astype(BF16)


def _fox_cumsum(f_logit, b_f_pad, tc):
    b, t, _ = f_logit.shape
    ltri = np.tril(np.ones((tc, tc), np.float32))
    place = np.zeros((3 * LANES, FOX_HEADS * LANES), np.float32)
    for hd in range(FOX_HEADS):
        for part in range(3):
            place[part * LANES + hd, hd * LANES + part] = 1.0
    return pl.pallas_call(
        functools.partial(_fox_cum_kernel, tc=tc),
        grid=(b, t // tc),
        in_specs=[pl.BlockSpec((1, tc, LANES), lambda i, j: (i, j, 0)),
                  pl.BlockSpec((1, LANES), lambda i, j: (0, 0)),
                  pl.BlockSpec((tc, tc), lambda i, j: (0, 0)),
                  pl.BlockSpec(place.shape, lambda i, j: (0, 0))],
        out_specs=pl.BlockSpec((1, tc, FOX_HEADS * LANES), lambda i, j: (i, j, 0)),
        out_shape=jax.ShapeDtypeStruct((b, t, FOX_HEADS * LANES), BF16),
        scratch_shapes=[pltpu.VMEM((1, LANES), F32)],
        compiler_params=_cparams("parallel", "arbitrary"),
        name="fox_cumsum",
    )(f_logit, b_f_pad, jnp.asarray(ltri, BF16), jnp.asarray(place, BF16))


def _fox_flash_kernel(q_ref, k_ref, cf_ref, vt_ref, o_ref, qp_ref, m_ref, l_ref, acc_ref, *, tq, tk):
    qi = pl.program_id(2)
    ki = pl.program_id(3)
    nkb = tq // tk
    lane = lax.broadcasted_iota(jnp.int32, (1, LANES), 1)

    @pl.when(ki == 0)
    def _():
        m_ref[...] = jnp.full_like(m_ref, -1e30)
        l_ref[...] = jnp.zeros_like(l_ref)
        acc_ref[...] = jnp.zeros_like(acc_ref)
        q = q_ref[0] * (FOX_DH ** -0.5)
        ones3 = jnp.broadcast_to(jnp.where(lane < 3, 1.0, 0.0).astype(BF16), (tq, LANES))
        for hh in range(2):
            qh = jnp.where((lane < FOX_DH) == (hh == 0), q, jnp.zeros_like(q))
            qp_ref[hh] = jnp.concatenate([qh, ones3], axis=1)

    def body(masked):
        k = k_ref[0]
        st = []
        for hh in range(2):
            kp = jnp.concatenate([k, cf_ref[0, :, hh * LANES:(hh + 1) * LANES]], axis=1)
            st.append(lax.dot_general(kp, qp_ref[hh], _NT, preferred_element_type=F32))
        if masked:
            key_pos = lax.broadcasted_iota(jnp.int32, (tk, tq), 0)
            qry_pos = lax.broadcasted_iota(jnp.int32, (tk, tq), 1)
            causal = key_pos + (ki - qi * nkb) * tk <= qry_pos
        for hh in range(2):
            s = jnp.where(causal, st[hh], -1e30) if masked else st[hh]
            m_prev = m_ref[hh]
            m_new = jnp.maximum(m_prev, jnp.max(s, axis=0, keepdims=True))
            alpha = jnp.exp(m_prev - m_new)
            p = jnp.exp(s - m_new)
            l_ref[hh] = alpha * l_ref[hh] + jnp.sum(p, axis=0, keepdims=True)
            acc_ref[hh] = alpha * acc_ref[hh] + jnp.dot(
                vt_ref[0, hh * FOX_DH:(hh + 1) * FOX_DH, :], p.astype(BF16), preferred_element_type=F32)
            m_ref[hh] = m_new

    @pl.when(ki < qi * nkb)
    def _():
        body(False)

    @pl.when(jnp.logical_and(ki >= qi * nkb, ki < (qi + 1) * nkb))
    def _():
        body(True)

    @pl.when(ki == (qi + 1) * nkb - 1)
    def _():
        o_ref[0] = jnp.concatenate([acc_ref[0] / l_ref[0], acc_ref[1] / l_ref[1]], axis=0).astype(o_ref.dtype)


def _fox_flash(qk, cfeat, v_t, tq, tk):
    b, t, _ = qk.shape
    npair = FOX_HEADS // 2
    nkb = tq // tk
    kv = lambda qi, ki: jnp.minimum(ki, (qi + 1) * nkb - 1)
    return pl.pallas_call(
        functools.partial(_fox_flash_kernel, tq=tq, tk=tk),
        grid=(b, npair, t // tq, t // tk),
        in_specs=[pl.BlockSpec((1, tq, LANES), lambda i, hp, qi, ki: (i, qi, hp)),
                  pl.BlockSpec((1, tk, LANES), lambda i, hp, qi, ki: (i, kv(qi, ki), npair + hp)),
                  pl.BlockSpec((1, tk, 2 * LANES), lambda i, hp, qi, ki: (i, kv(qi, ki), hp)),
                  pl.BlockSpec((1, LANES, tk), lambda i, hp, qi, ki: (i, hp, kv(qi, ki)))],
        out_specs=pl.BlockSpec((1, LANES, tq), lambda i, hp, qi, ki: (i, hp, qi)),
        out_shape=jax.ShapeDtypeStruct((b, D_MODEL, t), BF16),
        scratch_shapes=[pltpu.VMEM((2, tq, 2 * LANES), BF16),
                        pltpu.VMEM((2, 1, tq), F32),
                        pltpu.VMEM((2, 1, tq), F32),
                        pltpu.VMEM((2, FOX_DH, tq), F32)],
        compiler_params=_cparams("parallel", "parallel", "parallel", "arbitrary"),
        name="fox_flash",
    )(qk, qk, cfeat, v_t)


def _rms_proj_t_kernel(x_ref, g_ref, wt_ref, o_ref):
    xn = _rms(x_ref[0], g_ref[...]).astype(BF16)
    o_ref[0] = lax.dot_general(wt_ref[...], xn, _NT, preferred_element_type=F32).astype(o_ref.dtype)


def _rms_proj_t(x, g, wt_bf16, tm):
    b, t, d = x.shape
    f = wt_bf16.shape[0]
    return pl.pallas_call(
        _rms_proj_t_kernel,
        grid=(b, t // tm),
        in_specs=[pl.BlockSpec((1, tm, d), lambda i, j: (i, j, 0)),
                  pl.BlockSpec((1, d), lambda i, j: (0, 0)),
                  pl.BlockSpec((f, d), lambda i, j: (0, 0))],
        out_specs=pl.BlockSpec((1, f, tm), lambda i, j: (i, 0, j)),
        out_shape=jax.ShapeDtypeStruct((b, f, t), BF16),
        compiler_params=_cparams("parallel", "parallel"),
        name="rms_proj_t",
    )(x, g.reshape(1, d), wt_bf16)


def _mm_t_res_kernel(xt_ref, w_ref, r_ref, o_ref):
    o_ref[0] = r_ref[0] + lax.dot_general(xt_ref[0], w_ref[...], _TN, preferred_element_type=F32)


def _mm_t_res(x_t, w_bf16, res, tm):
    b, k, t = x_t.shape
    d = w_bf16.shape[1]
    return pl.pallas_call(
        _mm_t_res_kernel,
        grid=(b, t // tm),
        in_specs=[pl.BlockSpec((1, k, tm), lambda i, j: (i, 0, j)),
                  pl.BlockSpec((k, d), lambda i, j: (0, 0)),
                  pl.BlockSpec((1, tm, d), lambda i, j: (i, j, 0))],
        out_specs=pl.BlockSpec((1, tm, d), lambda i, j: (i, j, 0)),
        out_shape=jax.ShapeDtypeStruct((b, t, d), F32),
        compiler_params=_cparams("parallel", "parallel"),
        name="mm_t_res",
    )(x_t, w_bf16, res)


def _fox_layer(h, g_mix, w_in, b_f, w_out):
    b, t, d = h.shape
    n = b * t
    h2 = h.reshape(n, d)
    w_qk = w_in[:, :2 * d].astype(BF16)
    w_vt = w_in[:, 2 * d:3 * d].T.astype(BF16)
    w_f = jnp.pad(w_in[:, 3 * d:], ((0, 0), (0, LANES - FOX_HEADS))).astype(BF16)
    qk = _rms_proj(h2, g_mix, w_qk, BF16, min(PROJ_TOKENS, n), 1024).reshape(b, t, 2 * d)
    v_t = _rms_proj_t(h, g_mix, w_vt, min(512, t))
    fl = _rms_proj(h2, g_mix, w_f, F32, 512, LANES).reshape(b, t, LANES)
    b_f_pad = jnp.pad(b_f, (0, LANES - FOX_HEADS)).reshape(1, LANES)
    cfeat = _fox_cumsum(fl, b_f_pad, min(512, t))
    o_t = _fox_flash(qk, cfeat, v_t, min(FOX_TQ, t), min(FOX_TK, t))
    return _mm_t_res(o_t, w_out.astype(BF16), h, min(512, t))


def _softplus(x):
    return jnp.maximum(x, 0.0) + jnp.log1p(jnp.exp(-jnp.abs(x)))


def _ssd_kernel(p_ref, h_ref, cw_ref, cb_ref, dtb_ref, alog_ref, dex_ref, gn_ref, wout_ref,
                ltri_ref, eexp_ref, o_ref, buf_ref, st_ref, y_ref, *, c):
    di = SSM_DI
    gn_w = SSM_GROUPS * SSM_STATE
    gw = di // SSM_GROUPS

    @pl.when(pl.program_id(1) == 0)
    def _():
        buf_ref[0:8, :] = jnp.zeros((8, buf_ref.shape[1]), F32)
        st_ref[...] = jnp.zeros_like(st_ref)

    buf_ref[8:8 + c, :] = p_ref[0, :, di:di + di + 2 * gn_w]
    conv = cb_ref[...]
    for kk in range(SSM_CONV):
        conv = conv + cw_ref[kk:kk + 1, :] * buf_ref[5 + kk:5 + kk + c, :]
    buf_ref[0:8, :] = buf_ref[c:c + 8, :]
    xbc = _silu(conv)
    xs = xbc[:, :di]
    bm = xbc[:, di:di + gn_w]
    cm = xbc[:, di + gn_w:]

    dt = _softplus(p_ref[0, :, 2 * di + 2 * gn_w:] + dtb_ref[...])
    da = dt * (-jnp.exp(alog_ref[...]))
    cum = _dot01(ltri_ref[...], da)
    cum_t = cum.T
    ex = _dot01_r(jnp.concatenate([cum, dt], axis=0), eexp_ref[...])
    cum_e = ex[:c]
    dt_e = ex[c:]
    last = cum_e[c - 1:c]
    expcum = jnp.exp(cum_e)
    declast = jnp.exp(last)
    xdt = xs * dt_e
    xdt_b = xdt.astype(BF16)
    xdec_b = (xdt * jnp.exp(last - cum_e)).astype(BF16)

    row = lax.broadcasted_iota(jnp.int32, (c, c), 0)
    col = lax.broadcasted_iota(jnp.int32, (c, c), 1)
    tril = row >= col
    lane = lax.broadcasted_iota(jnp.int32, (1, LANES), 1)
    for g in range(SSM_GROUPS):
        cg = cm[:, g * SSM_STATE:(g + 1) * SSM_STATE].astype(BF16)
        bg32 = bm[:, g * SSM_STATE:(g + 1) * SSM_STATE]
        cb = lax.dot_general(cg, bg32.astype(BF16), _NT, preferred_element_type=F32)
        st = st_ref[g]
        y_inter = (jnp.dot(cg, st.astype(BF16), preferred_element_type=F32)
                   * expcum[:, g * gw:(g + 1) * gw])
        for pr in range(2):
            base = g * gw + pr * LANES
            xpair = xdt_b[:, base:base + LANES]
            outs = []
            for hh in range(2):
                hidx = g * 4 + pr * 2 + hh
                diff = cum[:, hidx:hidx + 1] - cum_t[hidx:hidx + 1, :]
                lmat = jnp.exp(jnp.where(tril, diff, -1e30))
                outs.append(jnp.dot((cb * lmat).astype(BF16), xpair, preferred_element_type=F32))
            y_intra = jnp.where(lane < SSM_HEADDIM, outs[0], outs[1])
            y_ref[:, base:base + LANES] = (y_intra + y_inter[:, pr * LANES:(pr + 1) * LANES]
                                           + xs[:, base:base + LANES] * dex_ref[:, base:base + LANES])
        st_ref[g] = st * declast[:, g * gw:(g + 1) * gw] + jnp.dot(
            bg32.T.astype(BF16), xdec_b[:, g * gw:(g + 1) * gw], preferred_element_type=F32)

    y = y_ref[...] * _silu(p_ref[0, :, 0:di])
    for g in range(SSM_GROUPS):
        sl = slice(g * gw, (g + 1) * gw)
        y_ref[:, sl] = _rms(y[:, sl], gn_ref[:, sl])
    o_ref[0] = h_ref[0] + jnp.dot(y_ref[...].astype(BF16), wout_ref[...], preferred_element_type=F32)


def _ssd_layer(h, g_mix, w_in, conv_w, conv_b, dt_bias, a_log, d_skip, g_norm, w_out):
    b, t, d = h.shape
    n = b * t
    c = min(SSM_CHUNK, t)
    di = SSM_DI
    cdim = di + 2 * SSM_GROUPS * SSM_STATE
    pad_h = LANES - SSM_HEADS
    w_all = jnp.pad(w_in, ((0, 0), (0, pad_h))).astype(BF16)
    fdim = w_all.shape[1]
    proj = _rms_proj(h.reshape(n, d), g_mix, w_all, F32, min(PROJ_TOKENS, n), 896).reshape(b, t, fdim)
    eexp = np.zeros((LANES, di), np.float32)
    for hd in range(SSM_HEADS):
        eexp[hd, hd * SSM_HEADDIM:(hd + 1) * SSM_HEADDIM] = 1.0
    ltri = np.tril(np.ones((c, c), np.float32))
    full = lambda shape: pl.BlockSpec(shape, lambda i, j: tuple(0 for _ in shape))
    return pl.pallas_call(
        functools.partial(_ssd_kernel, c=c),
        grid=(b, t // c),
        in_specs=[pl.BlockSpec((1, c, fdim), lambda i, j: (i, j, 0)),
                  pl.BlockSpec((1, c, d), lambda i, j: (i, j, 0)),
                  full((SSM_CONV, cdim)), full((1, cdim)), full((1, LANES)), full((1, LANES)),
                  full((1, di)), full((1, di)), full((di, d)), full((c, c)), full((LANES, di))],
        out_specs=pl.BlockSpec((1, c, d), lambda i, j: (i, j, 0)),
        out_shape=jax.ShapeDtypeStruct((b, t, d), F32),
        scratch_shapes=[pltpu.VMEM((c + 8, cdim), F32),
                        pltpu.VMEM((SSM_GROUPS, SSM_STATE, di // SSM_GROUPS), F32),
                        pltpu.VMEM((c, di), F32)],
        compiler_params=_cparams("parallel", "arbitrary"),
        name="ssd_scan",
    )(proj, h, conv_w, conv_b.reshape(1, cdim),
      jnp.pad(dt_bias, (0, pad_h)).reshape(1, LANES), jnp.pad(a_log, (0, pad_h)).reshape(1, LANES),
      jnp.repeat(d_skip, SSM_HEADDIM).reshape(1, di), g_norm.reshape(1, di), w_out.astype(BF16),
      jnp.asarray(ltri, BF16), jnp.asarray(eexp, BF16))


def _stack_rows(rows):
    iota16 = lax.broadcasted_iota(jnp.int32, (PEER_TOPK, LANES), 0)
    out = jnp.zeros((PEER_TOPK, LANES), F32)
    for it, r in enumerate(rows):
        out = jnp.where(iota16 == it, r, out)
    return out


def _top16(s, exact):
    nrows = s.shape[0]
    iota = lax.broadcasted_iota(jnp.int32, (nrows, LANES), 0).astype(F32)
    rank = jnp.full((nrows, LANES), float(PEER_TOPK), F32)
    vals = []
    for it in range(PEER_TOPK):
        m = jnp.max(s, axis=0, keepdims=True)
        sel = s == m
        if exact:
            sel = iota == jnp.min(jnp.where(sel, iota, float(nrows)), axis=0, keepdims=True)
        rank = jnp.where(sel, float(it), rank)
        s = jnp.where(sel, -jnp.inf, s)
        vals.append(m)
    return _stack_rows(vals), rank


def _top16_all(scores):
    fast = [_top16(s, False) for s in scores]
    picked = jnp.zeros((1, LANES), F32)
    for _, r in fast:
        picked = picked + jnp.sum(jnp.where(r < float(PEER_TOPK), 1.0, 0.0), axis=0, keepdims=True)
    tied = jnp.max(jnp.abs(picked - float(PEER_TOPK * len(scores)))) > 0.0
    flat = lax.cond(tied,
                    lambda: tuple(x for s in scores for x in _top16(s, True)),
                    lambda: tuple(x for vr in fast for x in vr))
    return [(flat[2 * i], flat[2 * i + 1]) for i in range(len(scores))]


def _row_pair_words(x, psel):
    y = lax.bitcast_convert_type(jnp.dot(psel, x, preferred_element_type=F32), jnp.uint32)
    half = x.shape[0] // 2
    return (y[:half] >> 16) | y[half:]


def _peer_route_kernel(h_ref, g_ref, wq_ref, keys_ref, psel_ref, xn_ref, r2_ref, e2_ref, na_ref, ca_ref,
                       xs_ref, q_ref, *, mt):
    @pl.when(pl.program_id(1) == 0)
    def _():
        xn = _rms(h_ref[...], g_ref[...]).astype(BF16)
        xs_ref[...] = xn
        xn_ref[...] = xn

    q_ref[...] = jnp.dot(xs_ref[...], wq_ref[...], preferred_element_type=F32).astype(BF16)
    k1 = keys_ref[0, 0]
    k2 = keys_ref[0, 1]

    nsub = mt // LANES
    s1s, s2s = [], []
    for sb in range(nsub):
        qs = q_ref[sb * LANES:(sb + 1) * LANES, :]
        s1s.append(lax.dot_general(k1, qs[:, :PEER_HALF], _NT, preferred_element_type=F32))
        s2s.append(lax.dot_general(k2, qs[:, PEER_HALF:], _NT, preferred_element_type=F32))
    tops = _top16_all(s1s + s2s)
    iota16 = lax.broadcasted_iota(jnp.int32, (PEER_TOPK, LANES), 0).astype(F32)

    cnts = [jnp.zeros((PEER_TOPK, LANES), F32) for _ in range(nsub)]
    fronts = [tops[sb][0] + tops[nsub + sb][0][0:1] for sb in range(nsub)]
    for _ in range(PEER_TOPK):
        for sb in range(nsub):
            sv1, sv2, front, cnt = tops[sb][0], tops[nsub + sb][0], fronts[sb], cnts[sb]
            m = jnp.max(front, axis=0, keepdims=True)
            idx = jnp.min(jnp.where(front == m, iota16, float(PEER_TOPK)), axis=0, keepdims=True)
            sel = iota16 == idx
            cnt = jnp.where(sel, cnt + 1.0, cnt)
            n_sel = jnp.max(jnp.where(sel, cnt, 0.0), axis=0, keepdims=True)
            nxt = jnp.max(jnp.where(iota16 == n_sel, sv2, -jnp.inf), axis=0, keepdims=True)
            fronts[sb] = jnp.where(sel, sv1 + nxt, front)
            cnts[sb] = cnt

    for sb in range(nsub):
        sv1, rank1 = tops[sb]
        sv2, rank2 = tops[nsub + sb]
        cnt = cnts[sb]
        s2 = s2s[sb]
        v2 = [sv2[j:j + 1] for j in range(PEER_TOPK)]
        e2r = jnp.exp(sv2 - v2[0])
        pref = jnp.zeros((1, LANES), F32)
        pn = jnp.zeros((PEER_TOPK, LANES), F32)
        for j in range(PEER_TOPK):
            pref = pref + e2r[j:j + 1]
            pn = jnp.where(cnt == float(j + 1), pref, pn)
        e1 = jnp.exp(sv1 - sv1[0:1])
        z = jnp.sum(e1 * pn, axis=0, keepdims=True)
        cw = e1 * (1.0 / z)
        na = jnp.zeros((PEER_NKEYS, LANES), F32)
        ca = jnp.zeros((PEER_NKEYS, LANES), F32)
        for i in range(PEER_TOPK):
            hit = rank1 == float(i)
            na = jnp.where(hit, cnt[i:i + 1], na)
            ca = jnp.where(hit, cw[i:i + 1], ca)
        r2_ref[sb, 0] = _row_pair_words(rank2.astype(BF16), psel_ref[...])
        e2_ref[sb, 0] = _row_pair_words(jnp.exp(s2 - v2[0]).astype(BF16), psel_ref[...])
        na_ref[sb, 0] = _pair_words(na)
        ca_ref[sb, 0] = _pair_words(ca)


def _pair_words(x):
    u = lax.bitcast_convert_type(x.astype(BF16).astype(F32), jnp.uint32)
    return u | (u >> 16)


def _packed_rows(words):
    one = pltpu.bitcast(jnp.broadcast_to(words, (BF16_SUBLANES // 2, LANES)), BF16)
    return jnp.concatenate([one] * (LANES // BF16_SUBLANES), axis=0)


def _gelu(x):
    return 0.5 * x * (1.0 + lax.erf(x * (0.5 ** 0.5)))


def _peer_dense_kernel(xn_ref, u_ref, vt_ref, r2_ref, e2_ref, na_ref, ca_ref, h_ref, o_ref,
                       acc_ref, hida_ref, hidb_ref, xua_ref, xub_ref, *, mt, eb, n_eblk):
    e = pl.program_id(0)
    n_pairs = pl.num_programs(0) - 2
    e_gate = lax.rem(jnp.clip(e - 1, 0, n_pairs - 1), n_eblk)
    e_v = lax.rem(jnp.clip(e - 2, 0, n_pairs - 1), n_eblk)

    @pl.when(e == 0)
    def _():
        hida_ref[...] = jnp.zeros_like(hida_ref)
        hidb_ref[...] = jnp.zeros_like(hidb_ref)
        xub_ref[...] = jnp.zeros_like(xub_ref)

    @pl.when(e_v == 0)
    def _():
        acc_ref[...] = jnp.zeros_like(acc_ref)

    zero = jnp.zeros((LANES, LANES), BF16)
    nchunk = eb // PEER_CHUNK
    dch = acc_ref.shape[0] // nchunk

    def step(xu_new, xu_old, hid_new, hid_old):
        def v_matmul(j):
            acc_ref[j * dch:(j + 1) * dch, :] += jnp.dot(vt_ref[0, j * dch:(j + 1) * dch, :], hid_old[...],
                                                         preferred_element_type=F32)

        def u_matmul(j):
            rows = slice(j * PEER_CHUNK, (j + 1) * PEER_CHUNK)
            xu_new[rows, :] = lax.dot_general(u_ref[rows, :], xn_ref[...], _NT,
                                              preferred_element_type=F32)

        def gate_tile(ablk, sb):
            gs = zero
            for hd in range(PEER_HEADS):
                r2 = pltpu.bitcast(r2_ref[sb, hd], BF16)
                e2 = pltpu.bitcast(e2_ref[sb, hd], BF16)
                n_row = _packed_rows(na_ref[sb, hd, 0, ablk:ablk + 1, :])
                c_row = _packed_rows(ca_ref[sb, hd, 0, ablk:ablk + 1, :])
                gs = gs + jnp.where(r2 < n_row, e2, zero) * c_row
            r0 = ablk * LANES
            act = _gelu(xu_old[r0:r0 + LANES, sb * LANES:(sb + 1) * LANES])
            hid_new[r0:r0 + LANES, sb * LANES:(sb + 1) * LANES] = act.astype(BF16) * gs

        nsub = mt // LANES
        mxu_parts = [f for j in range(nchunk) for f in (functools.partial(v_matmul, j),
                                                        functools.partial(u_matmul, j))]
        tiles = [(ablk, sb) for ablk in range(eb // LANES) for sb in range(nsub)]
        per_part = len(tiles) // len(mxu_parts)
        for i, part in enumerate(mxu_parts):
            part()
            for ablk, sb in tiles[i * per_part:(i + 1) * per_part]:
                gate_tile(ablk, sb)

    @pl.when(e % 2 == 0)
    def _():
        step(xua_ref, xub_ref, hidb_ref, hida_ref)

    @pl.when(e % 2 == 1)
    def _():
        step(xub_ref, xua_ref, hida_ref, hidb_ref)

    @pl.when(jnp.logical_and(e_v == n_eblk - 1, e >= 2))
    def _():
        o_ref[...] = h_ref[...] + acc_ref[...].T


def _peer_layer(h2, g_ffn, w_q, keys, u_tab, v_tab, mt, mt_dense, eb):
    n, d = h2.shape
    n_eblk = u_tab.shape[0] // eb
    nsub = mt // LANES
    half = PEER_NKEYS // 2
    rshape_w = jax.ShapeDtypeStruct((n // LANES, PEER_HEADS, PEER_NKEYS, LANES), jnp.uint32)
    rshape_p = jax.ShapeDtypeStruct((n // LANES, PEER_HEADS, half, LANES), jnp.uint32)
    rspec = pl.BlockSpec((nsub, 1, PEER_NKEYS, LANES), lambda i, hd: (i, hd, 0, 0))
    pspec = pl.BlockSpec((nsub, 1, half, LANES), lambda i, hd: (i, hd, 0, 0))
    psel = np.zeros((PEER_NKEYS, PEER_NKEYS), np.float32)
    psel[np.arange(half), 2 * np.arange(half)] = 1.0
    psel[half + np.arange(half), 2 * np.arange(half) + 1] = 1.0
    xn, r2, e2, na, ca = pl.pallas_call(
        functools.partial(_peer_route_kernel, mt=mt),
        grid=(n // mt, PEER_HEADS),
        in_specs=[pl.BlockSpec((mt, d), lambda i, hd: (i, 0)),
                  pl.BlockSpec((1, d), lambda i, hd: (0, 0)),
                  pl.BlockSpec((d, 2 * PEER_HALF), lambda i, hd: (0, hd)),
                  pl.BlockSpec((1, 2, PEER_NKEYS, PEER_HALF), lambda i, hd: (hd, 0, 0, 0)),
                  pl.BlockSpec((PEER_NKEYS, PEER_NKEYS), lambda i, hd: (0, 0))],
        out_specs=[pl.BlockSpec((mt, d), lambda i, hd: (i, 0)), pspec, pspec, rspec, rspec],
        out_shape=[jax.ShapeDtypeStruct((n, d), BF16), rshape_p, rshape_p, rshape_w, rshape_w],
        scratch_shapes=[pltpu.VMEM((mt, d), BF16), pltpu.VMEM((mt, 2 * PEER_HALF), BF16)],
        compiler_params=_cparams("parallel", "arbitrary"),
        name="peer_route",
    )(h2, g_ffn.reshape(1, d), w_q.astype(BF16), keys.astype(BF16), jnp.asarray(psel, BF16))

    mtd = mt_dense
    nsub_d = mtd // LANES
    n_pairs = (n // mtd) * n_eblk
    pair = lambda s, lag: jnp.clip(s - lag, 0, n_pairs - 1)
    n_ablk = eb // LANES
    blocked = lambda x: x.reshape(n // LANES, PEER_HEADS, n_eblk, n_ablk, LANES)
    dspec = pl.BlockSpec((nsub_d, PEER_HEADS, 1, n_ablk, LANES),
                         lambda s: (pair(s, 1) // n_eblk, 0, pair(s, 1) % n_eblk, 0, 0))
    dspec_p = pl.BlockSpec((nsub_d, PEER_HEADS, half, LANES), lambda s: (pair(s, 1) // n_eblk, 0, 0, 0))
    v_blocks = v_tab.astype(BF16).reshape(n_eblk, eb, d).transpose(0, 2, 1)
    return pl.pallas_call(
        functools.partial(_peer_dense_kernel, mt=mtd, eb=eb, n_eblk=n_eblk),
        grid=(n_pairs + 2,),
        in_specs=[pl.BlockSpec((mtd, d), lambda s: (pair(s, 0) // n_eblk, 0)),
                  pl.BlockSpec((eb, d), lambda s: (pair(s, 0) % n_eblk, 0)),
                  pl.BlockSpec((1, d, eb), lambda s: (pair(s, 2) % n_eblk, 0, 0)),
                  dspec_p, dspec_p, dspec, dspec,
                  pl.BlockSpec((mtd, d), lambda s: (pair(s, 2) // n_eblk, 0))],
        out_specs=pl.BlockSpec((mtd, d), lambda s: (pair(s, 2) // n_eblk, 0)),
        out_shape=jax.ShapeDtypeStruct((n, d), F32),
        scratch_shapes=[pltpu.VMEM((d, mtd), F32), pltpu.VMEM((eb, mtd), BF16),
                        pltpu.VMEM((eb, mtd), BF16), pltpu.VMEM((eb, mtd), F32), pltpu.VMEM((eb, mtd), F32)],
        compiler_params=_cparams("arbitrary"),
        name="peer_dense",
    )(xn, u_tab.astype(BF16), v_blocks, r2, e2, blocked(na), blocked(ca), h2)


def kernel(x, norm_mix, norm_ffn, norm_final, hgrn_lb_logits, hgrn_w_in, hgrn_gnorm, hgrn_w_out,
           fox_w_in, fox_b_f, fox_w_out, ssm_w_in, ssm_conv_w, ssm_conv_b, ssm_dt_bias, ssm_a_log,
           ssm_d, ssm_gnorm, ssm_w_out, peer_w_q, peer_keys, peer_u, peer_v):
    b, t, d = x.shape
    n = b * t
    depth = norm_mix.shape[0]
    h = x
    for i in range(depth):
        kind, j = i % 3, i // 3
        if kind == 0:
            proj = _rms_proj(h.reshape(n, d), norm_mix[i], hgrn_w_in[j].astype(BF16), F32,
                             min(PROJ_TOKENS, n), 1024)
            h = _hgrn_layer(h, proj.reshape(b, t, 4 * d), hgrn_lb_logits, hgrn_gnorm[j],
                            hgrn_w_out[j].astype(BF16), i)
        elif kind == 1:
            h = _fox_layer(h, norm_mix[i], fox_w_in[j], fox_b_f[j], fox_w_out[j])
        else:
            h = _ssd_layer(h, norm_mix[i], ssm_w_in[j], ssm_conv_w[j], ssm_conv_b[j], ssm_dt_bias[j],
                           ssm_a_log[j], ssm_d[j], ssm_gnorm[j], ssm_w_out[j])
        h = _peer_layer(h.reshape(n, d), norm_ffn[i], peer_w_q[i], peer_keys[i], peer_u[i],
                        peer_v[i], min(PEER_ROUTE_TOKENS, n), min(PEER_DENSE_TOKENS, n), PEER_DENSE_EXPERTS).reshape(b, t, d)
    return _final_norm(h.reshape(n, d), norm_final, 512).reshape(b, t, d)
```

```python
import functools
import math

import numpy as np
import jax
import jax.numpy as jnp
from jax import lax
from jax.experimental import pallas as pl
from jax.experimental.pallas import tpu as pltpu

F32 = jnp.float32
BF16 = jnp.bfloat16

NORM_EPS = 1e-6
LANES = 128
BF16_SUBLANES = 16
VMEM_LIMIT = 56 * 1024 * 1024
PROJ_TOKENS = 1024

D_MODEL = 1024
HG_HEADS = 8
HG_DK = 128
GLA_CHUNK = 128
GLA_SUB = 16
FOX_HEADS = 16
FOX_DH = 64
FOX_TQ = 1024
FOX_TK = 1024
SSM_DI = 2048
SSM_HEADS = 32
SSM_HEADDIM = 64
SSM_GROUPS = 8
SSM_STATE = 128
SSM_CONV = 4
SSM_CHUNK = 128
PEER_HEADS = 8
PEER_NKEYS = 128
PEER_TOPK = 16
PEER_HALF = 128
PEER_ROUTE_TOKENS = 512
PEER_DENSE_TOKENS = 1024
PEER_DENSE_EXPERTS = 512
PEER_CHUNK = 256

_NT = (((1,), (1,)), ((), ()))
_TN = (((0,), (0,)), ((), ()))


def _cparams(*sem):
    return pltpu.CompilerParams(dimension_semantics=sem, vmem_limit_bytes=VMEM_LIMIT)


def _split3(x):
    hi = x.astype(BF16)
    r = x - hi.astype(F32)
    mid = r.astype(BF16)
    lo = (r - mid.astype(F32)).astype(BF16)
    return hi, mid, lo


def _dot01(m01, x):
    hi, mid, lo = _split3(x)
    d = lambda b: jnp.dot(m01, b, preferred_element_type=F32)
    return d(hi) + d(mid) + d(lo)


def _dot01_r(x, m01):
    hi, mid, lo = _split3(x)
    d = lambda a: jnp.dot(a, m01, preferred_element_type=F32)
    return d(hi) + d(mid) + d(lo)


def _sigmoid(x):
    return 1.0 / (1.0 + jnp.exp(-x))


def _silu(x):
    return x * _sigmoid(x)


def _rms(x, g):
    ms = jnp.mean(x * x, axis=-1, keepdims=True)
    return x * lax.rsqrt(ms + NORM_EPS) * g


def _rms_proj_kernel(x_ref, g_ref, w_ref, o_ref, xn_ref):
    @pl.when(pl.program_id(1) == 0)
    def _():
        xn_ref[...] = _rms(x_ref[...], g_ref[...]).astype(BF16)

    o_ref[...] = jnp.dot(xn_ref[...], w_ref[...], preferred_element_type=F32).astype(o_ref.dtype)


def _rms_proj(x2d, g, w_bf16, out_dtype, tm, tn):
    n, d = x2d.shape
    f = w_bf16.shape[1]
    assert n % tm == 0 and f % tn == 0
    return pl.pallas_call(
        _rms_proj_kernel,
        grid=(n // tm, f // tn),
        in_specs=[pl.BlockSpec((tm, d), lambda i, j: (i, 0)),
                  pl.BlockSpec((1, d), lambda i, j: (0, 0)),
                  pl.BlockSpec((d, tn), lambda i, j: (0, j))],
        out_specs=pl.BlockSpec((tm, tn), lambda i, j: (i, j)),
        out_shape=jax.ShapeDtypeStruct((n, f), out_dtype),
        scratch_shapes=[pltpu.VMEM((tm, d), BF16)],
        compiler_params=_cparams("parallel", "arbitrary"),
        name="rms_proj",
    )(x2d, g.reshape(1, d), w_bf16)


def _mm_res_kernel(x_ref, w_ref, r_ref, o_ref):
    o_ref[...] = r_ref[...] + jnp.dot(x_ref[...], w_ref[...], preferred_element_type=F32)


def _mm_res(x2d_bf16, w_bf16, res2d, tm):
    n, k = x2d_bf16.shape
    d = w_bf16.shape[1]
    return pl.pallas_call(
        _mm_res_kernel,
        grid=(n // tm,),
        in_specs=[pl.BlockSpec((tm, k), lambda i: (i, 0)),
                  pl.BlockSpec((k, d), lambda i: (0, 0)),
                  pl.BlockSpec((tm, d), lambda i: (i, 0))],
        out_specs=pl.BlockSpec((tm, d), lambda i: (i, 0)),
        out_shape=jax.ShapeDtypeStruct((n, d), F32),
        compiler_params=_cparams("parallel"),
        name="mm_res",
    )(x2d_bf16, w_bf16, res2d)


def _final_norm_kernel(x_ref, g_ref, o_ref):
    o_ref[...] = _rms(x_ref[...], g_ref[...])


def _final_norm(x2d, g, tm):
    n, d = x2d.shape
    return pl.pallas_call(
        _final_norm_kernel,
        grid=(n // tm,),
        in_specs=[pl.BlockSpec((tm, d), lambda i: (i, 0)),
                  pl.BlockSpec((1, d), lambda i: (0, 0))],
        out_specs=pl.BlockSpec((tm, d), lambda i: (i, 0)),
        out_shape=jax.ShapeDtypeStruct((n, d), F32),
        compiler_params=_cparams("parallel"),
        name="final_norm",
    )(x2d, g.reshape(1, d))


def _gla_tables(c, sub):
    t = np.arange(c)[:, None]
    j = np.arange(c)[None, :]
    mats = [(j <= t).astype(np.float32)]
    masks = []
    bk = c // 2
    while bk >= sub:
        blk = t // bk
        odd = (blk % 2) == 1
        mats.append(((odd & (j >= blk * bk) & (j <= t))
                     | ((~odd) & (j > t) & (j < (blk + 1) * bk))).astype(np.float32))
        masks.append((odd & ((j // bk) == blk - 1)).astype(np.float32))
        bk //= 2
    m0 = (t // sub) * sub
    mats.append(((j >= m0) & (j <= t)).astype(np.float32))
    masks.append((((t // sub) == (j // sub)) & (j <= t)).astype(np.float32))
    return np.concatenate(mats, axis=0), np.stack(masks, axis=0)


def _hgrn_kernel(proj_ref, h_ref, lbl_ref, gn_ref, wout_ref, dst_ref, msk_ref, o_ref,
                 st_ref, y_ref, *, layer, depth, c, nlev):
    @pl.when(pl.program_id(1) == 0)
    def _():
        st_ref[...] = jnp.zeros_like(st_ref)

    rows = [lbl_ref[j:j + 1, :] for j in range(depth)]
    mx = functools.reduce(jnp.maximum, rows)
    es = [jnp.exp(r - mx) for r in rows]
    tot = functools.reduce(lambda a, b: a + b, es)
    lb = jnp.zeros_like(mx)
    for j in range(1, layer + 1):
        lb = lb + es[j] / tot

    d = HG_HEADS * HG_DK
    q = proj_ref[0, :, 0:d] * (HG_DK ** -0.5)
    f = lb + (1.0 - lb) * _sigmoid(proj_ref[0, :, d:2 * d])
    k = 1.0 - f
    expo = _dot01(dst_ref[...], jnp.log(f))
    b_cum = expo[0:c]
    q_dec = (q * jnp.exp(b_cum)).astype(BF16)
    k_dec = (k * jnp.exp(b_cum[c - 1:c] - b_cum)).astype(BF16)
    dec_last = jnp.exp(b_cum[c - 1:c])
    qs, ks = [], []
    for l in range(nlev - 1):
        w = jnp.exp(expo[(1 + l) * c:(2 + l) * c])
        qs.append((q * w).astype(BF16))
        ks.append((k * w).astype(BF16))
    e_diag = expo[nlev * c:(nlev + 1) * c]
    qs.append((q * jnp.exp(e_diag)).astype(BF16))
    ks.append((k * jnp.exp(-e_diag)).astype(BF16))
    v = proj_ref[0, :, 2 * d:3 * d].astype(BF16)
    gate = proj_ref[0, :, 3 * d:4 * d]
    gn = gn_ref[...]
    for hd in range(HG_HEADS):
        sl = slice(hd * HG_DK, (hd + 1) * HG_DK)
        a = jnp.zeros((c, c), F32)
        for l in range(nlev):
            a = a + lax.dot_general(qs[l][:, sl], ks[l][:, sl], _NT,
                                    preferred_element_type=F32) * msk_ref[l]
        st = st_ref[hd]
        o = (jnp.dot(a.astype(BF16), v[:, sl], preferred_element_type=F32)
             + lax.dot_general(q_dec[:, sl], st.astype(BF16), _NT, preferred_element_type=F32))
        st_ref[hd] = st * dec_last[:, sl] + lax.dot_general(
            v[:, sl], k_dec[:, sl], _TN, preferred_element_type=F32)
        y_ref[:, sl] = (_rms(o, gn) * _silu(gate[:, sl])).astype(BF16)
    o_ref[0] = h_ref[0] + jnp.dot(y_ref[...], wout_ref[...], preferred_element_type=F32)


def _hgrn_layer(h, proj, lb_logits, g_norm, w_out_bf16, layer):
    b, t, d = h.shape
    c = GLA_CHUNK
    dst, msk = _gla_tables(c, GLA_SUB)
    nlev = msk.shape[0]
    depth = lb_logits.shape[0]
    kern = functools.partial(_hgrn_kernel, layer=layer, depth=depth, c=c, nlev=nlev)
    return pl.pallas_call(
        kern,
        grid=(b, t // c),
        in_specs=[pl.BlockSpec((1, c, 4 * d), lambda i, j: (i, j, 0)),
                  pl.BlockSpec((1, c, d), lambda i, j: (i, j, 0)),
                  pl.BlockSpec((depth, d), lambda i, j: (0, 0)),
                  pl.BlockSpec((1, HG_DK), lambda i, j: (0, 0)),
                  pl.BlockSpec((d, d), lambda i, j: (0, 0)),
                  pl.BlockSpec(dst.shape, lambda i, j: (0, 0)),
                  pl.BlockSpec(msk.shape, lambda i, j: (0, 0, 0))],
        out_specs=pl.BlockSpec((1, c, d), lambda i, j: (i, j, 0)),
        out_shape=jax.ShapeDtypeStruct((b, t, d), F32),
        scratch_shapes=[pltpu.VMEM((HG_HEADS, HG_DK, HG_DK), F32),
                        pltpu.VMEM((c, d), BF16)],
        compiler_params=_cparams("parallel", "arbitrary"),
        name="hgrn_gla",
    )(proj, h, lb_logits, g_norm.reshape(1, HG_DK), w_out_bf16,
      jnp.asarray(dst, BF16), jnp.asarray(msk, F32))


def _log_sigmoid(x):
    return jnp.minimum(x, 0.0) - jnp.log1p(jnp.exp(-jnp.abs(x)))


def _fox_cum_kernel(fl_ref, bf_ref, ltri_ref, place_ref, o_ref, carry_ref, *, tc):
    @pl.when(pl.program_id(1) == 0)
    def _():
        carry_ref[...] = jnp.zeros_like(carry_ref)

    lf = _log_sigmoid(fl_ref[0] + bf_ref[...])
    c = _dot01(ltri_ref[...], lf) + carry_ref[...]
    carry_ref[...] = c[tc - 1:tc, :]
    hi, mid, lo = _split3(-c)
    o_ref[0] = jnp.dot(jnp.concatenate([hi, mid, lo], axis=1), place_ref[...],
                       preferred_element_type=F32).astype(BF16)


def _fox_cumsum(f_logit, b_f_pad, tc):
    b, t, _ = f_logit.shape
    ltri = np.tril(np.ones((tc, tc), np.float32))
    place = np.zeros((3 * LANES, FOX_HEADS * LANES), np.float32)
    for hd in range(FOX_HEADS):
        for part in range(3):
            place[part * LANES + hd, hd * LANES + part] = 1.0
    return pl.pallas_call(
        functools.partial(_fox_cum_kernel, tc=tc),
        grid=(b, t // tc),
        in_specs=[pl.BlockSpec((1, tc, LANES), lambda i, j: (i, j, 0)),
                  pl.BlockSpec((1, LANES), lambda i, j: (0, 0)),
                  pl.BlockSpec((tc, tc), lambda i, j: (0, 0)),
                  pl.BlockSpec(place.shape, lambda i, j: (0, 0))],
        out_specs=pl.BlockSpec((1, tc, FOX_HEADS * LANES), lambda i, j: (i, j, 0)),
        out_shape=jax.ShapeDtypeStruct((b, t, FOX_HEADS * LANES), BF16),
        scratch_shapes=[pltpu.VMEM((1, LANES), F32)],
        compiler_params=_cparams("parallel", "arbitrary"),
        name="fox_cumsum",
    )(f_logit, b_f_pad, jnp.asarray(ltri, BF16), jnp.asarray(place, BF16))


def _fox_flash_kernel(q_ref, k_ref, cf_ref, vt_ref, o_ref, qp_ref, m_ref, l_ref, acc_ref, *, tq, tk):
    qi = pl.program_id(2)
    ki = pl.program_id(3)
    nkb = tq // tk
    lane = lax.broadcasted_iota(jnp.int32, (1, LANES), 1)

    @pl.when(ki == 0)
    def _():
        m_ref[...] = jnp.full_like(m_ref, -1e30)
        l_ref[...] = jnp.zeros_like(l_ref)
        acc_ref[...] = jnp.zeros_like(acc_ref)
        q = q_ref[0] * (FOX_DH ** -0.5)
        ones3 = jnp.broadcast_to(jnp.where(lane < 3, 1.0, 0.0).astype(BF16), (tq, LANES))
        for hh in range(2):
            qh = jnp.where((lane < FOX_DH) == (hh == 0), q, jnp.zeros_like(q))
            qp_ref[hh] = jnp.concatenate([qh, ones3], axis=1)

    def body(masked):
        k = k_ref[0]
        st = []
        for hh in range(2):
            kp = jnp.concatenate([k, cf_ref[0, :, hh * LANES:(hh + 1) * LANES]], axis=1)
            st.append(lax.dot_general(kp, qp_ref[hh], _NT, preferred_element_type=F32))
        if masked:
            key_pos = lax.broadcasted_iota(jnp.int32, (tk, tq), 0)
            qry_pos = lax.broadcasted_iota(jnp.int32, (tk, tq), 1)
            if nkb > 1:
                key_pos = key_pos + (ki - qi * nkb) * tk
            causal = key_pos <= qry_pos
        for hh in range(2):
            s = jnp.where(causal, st[hh], -1e30) if masked else st[hh]
            m_prev = m_ref[hh]
            m_new = jnp.maximum(m_prev, jnp.max(s, axis=0, keepdims=True))
            alpha = jnp.exp(m_prev - m_new)
            p = jnp.exp(s - m_new)
            l_ref[hh] = alpha * l_ref[hh] + jnp.sum(p, axis=0, keepdims=True)
            acc_ref[hh] = alpha * acc_ref[hh] + jnp.dot(
                vt_ref[0, hh * FOX_DH:(hh + 1) * FOX_DH, :], p.astype(BF16), preferred_element_type=F32)
            m_ref[hh] = m_new

    @pl.when(ki < qi * nkb)
    def _():
        body(False)

    @pl.when(jnp.logical_and(ki >= qi * nkb, ki < (qi + 1) * nkb))
    def _():
        body(True)

    @pl.when(ki == (qi + 1) * nkb - 1)
    def _():
        o_ref[0] = jnp.concatenate([acc_ref[0] / l_ref[0], acc_ref[1] / l_ref[1]], axis=0).astype(o_ref.dtype)


def _fox_flash(qk, cfeat, v_t, tq, tk):
    b, t, _ = qk.shape
    npair = FOX_HEADS // 2
    nkb = tq // tk
    kv = lambda qi, ki: jnp.minimum(ki, (qi + 1) * nkb - 1)
    return pl.pallas_call(
        functools.partial(_fox_flash_kernel, tq=tq, tk=tk),
        grid=(b, npair, t // tq, t // tk),
        in_specs=[pl.BlockSpec((1, tq, LANES), lambda i, hp, qi, ki: (i, qi, hp)),
                  pl.BlockSpec((1, tk, LANES), lambda i, hp, qi, ki: (i, kv(qi, ki), npair + hp)),
                  pl.BlockSpec((1, tk, 2 * LANES), lambda i, hp, qi, ki: (i, kv(qi, ki), hp)),
                  pl.BlockSpec((1, LANES, tk), lambda i, hp, qi, ki: (i, hp, kv(qi, ki)))],
        out_specs=pl.BlockSpec((1, LANES, tq), lambda i, hp, qi, ki: (i, hp, qi)),
        out_shape=jax.ShapeDtypeStruct((b, D_MODEL, t), BF16),
        scratch_shapes=[pltpu.VMEM((2, tq, 2 * LANES), BF16),
                        pltpu.VMEM((2, 1, tq), F32),
                        pltpu.VMEM((2, 1, tq), F32),
                        pltpu.VMEM((2, FOX_DH, tq), F32)],
        compiler_params=_cparams("parallel", "parallel", "parallel", "arbitrary"),
        name="fox_flash",
    )(qk, qk, cfeat, v_t)


def _rms_proj_t_kernel(x_ref, g_ref, wt_ref, o_ref):
    xn = _rms(x_ref[0], g_ref[...]).astype(BF16)
    o_ref[0] = lax.dot_general(wt_ref[...], xn, _NT, preferred_element_type=F32).astype(o_ref.dtype)


def _rms_proj_t(x, g, wt_bf16, tm):
    b, t, d = x.shape
    f = wt_bf16.shape[0]
    return pl.pallas_call(
        _rms_proj_t_kernel,
        grid=(b, t // tm),
        in_specs=[pl.BlockSpec((1, tm, d), lambda i, j: (i, j, 0)),
                  pl.BlockSpec((1, d), lambda i, j: (0, 0)),
                  pl.BlockSpec((f, d), lambda i, j: (0, 0))],
        out_specs=pl.BlockSpec((1, f, tm), lambda i, j: (i, 0, j)),
        out_shape=jax.ShapeDtypeStruct((b, f, t), BF16),
        compiler_params=_cparams("parallel", "parallel"),
        name="rms_proj_t",
    )(x, g.reshape(1, d), wt_bf16)


def _mm_t_res_kernel(xt_ref, w_ref, r_ref, o_ref):
    o_ref[0] = r_ref[0] + lax.dot_general(xt_ref[0], w_ref[...], _TN, preferred_element_type=F32)


def _mm_t_res(x_t, w_bf16, res, tm):
    b, k, t = x_t.shape
    d = w_bf16.shape[1]
    return pl.pallas_call(
        _mm_t_res_kernel,
        grid=(b, t // tm),
        in_specs=[pl.BlockSpec((1, k, tm), lambda i, j: (i, 0, j)),
                  pl.BlockSpec((k, d), lambda i, j: (0, 0)),
                  pl.BlockSpec((1, tm, d), lambda i, j: (i, j, 0))],
        out_specs=pl.BlockSpec((1, tm, d), lambda i, j: (i, j, 0)),
        out_shape=jax.ShapeDtypeStruct((b, t, d), F32),
        compiler_params=_cparams("parallel", "parallel"),
        name="mm_t_res",
    )(x_t, w_bf16, res)


def _fox_layer(h, g_mix, w_in, b_f, w_out):
    b, t, d = h.shape
    n = b * t
    h2 = h.reshape(n, d)
    w_qk = w_in[:, :2 * d].astype(BF16)
    w_vt = w_in[:, 2 * d:3 * d].T.astype(BF16)
    w_f = jnp.pad(w_in[:, 3 * d:], ((0, 0), (0, LANES - FOX_HEADS))).astype(BF16)
    qk = _rms_proj(h2, g_mix, w_qk, BF16, min(PROJ_TOKENS, n), 1024).reshape(b, t, 2 * d)
    v_t = _rms_proj_t(h, g_mix, w_vt, min(512, t))
    fl = _rms_proj(h2, g_mix, w_f, F32, 512, LANES).reshape(b, t, LANES)
    b_f_pad = jnp.pad(b_f, (0, LANES - FOX_HEADS)).reshape(1, LANES)
    cfeat = _fox_cumsum(fl, b_f_pad, min(512, t))
    o_t = _fox_flash(qk, cfeat, v_t, min(FOX_TQ, t), min(FOX_TK, t))
    return _mm_t_res(o_t, w_out.astype(BF16), h, min(512, t))


def _softplus(x):
    return jnp.maximum(x, 0.0) + jnp.log1p(jnp.exp(-jnp.abs(x)))


def _ssd_kernel(p_ref, h_ref, cw_ref, cb_ref, dtb_ref, alog_ref, dex_ref, gn_ref, wout_ref,
                ltri_ref, eexp_ref, o_ref, buf_ref, st_ref, y_ref, *, c):
    di = SSM_DI
    gn_w = SSM_GROUPS * SSM_STATE
    gw = di // SSM_GROUPS

    @pl.when(pl.program_id(1) == 0)
    def _():
        buf_ref[0:8, :] = jnp.zeros((8, buf_ref.shape[1]), F32)
        st_ref[...] = jnp.zeros_like(st_ref)

    buf_ref[8:8 + c, :] = p_ref[0, :, di:di + di + 2 * gn_w]
    conv = cb_ref[...]
    for kk in range(SSM_CONV):
        conv = conv + cw_ref[kk:kk + 1, :] * buf_ref[5 + kk:5 + kk + c, :]
    buf_ref[0:8, :] = buf_ref[c:c + 8, :]
    xbc = _silu(conv)
    xs = xbc[:, :di]
    bm = xbc[:, di:di + gn_w]
    cm = xbc[:, di + gn_w:]

    dt = _softplus(p_ref[0, :, 2 * di + 2 * gn_w:] + dtb_ref[...])
    da = dt * (-jnp.exp(alog_ref[...]))
    cum = _dot01(ltri_ref[...], da)
    cum_t = cum.T
    ex = _dot01_r(jnp.concatenate([cum, dt], axis=0), eexp_ref[...])
    cum_e = ex[:c]
    dt_e = ex[c:]
    last = cum_e[c - 1:c]
    expcum = jnp.exp(cum_e)
    declast = jnp.exp(last)
    xdt = xs * dt_e
    xdt_b = xdt.astype(BF16)
    xdec_b = (xdt * jnp.exp(last - cum_e)).astype(BF16)

    row = lax.broadcasted_iota(jnp.int32, (c, c), 0)
    col = lax.broadcasted_iota(jnp.int32, (c, c), 1)
    tril = row >= col
    lane = lax.broadcasted_iota(jnp.int32, (1, LANES), 1)
    for g in range(SSM_GROUPS):
        cg = cm[:, g * SSM_STATE:(g + 1) * SSM_STATE].astype(BF16)
        bg32 = bm[:, g * SSM_STATE:(g + 1) * SSM_STATE]
        cb = lax.dot_general(cg, bg32.astype(BF16), _NT, preferred_element_type=F32)
        st = st_ref[g]
        y_inter = (jnp.dot(cg, st.astype(BF16), preferred_element_type=F32)
                   * expcum[:, g * gw:(g + 1) * gw])
        for pr in range(2):
            base = g * gw + pr * LANES
            xpair = xdt_b[:, base:base + LANES]
            outs = []
            for hh in range(2):
                hidx = g * 4 + pr * 2 + hh
                diff = cum[:, hidx:hidx + 1] - cum_t[hidx:hidx + 1, :]
                lmat = jnp.exp(jnp.where(tril, diff, -1e30))
                outs.append(jnp.dot((cb * lmat).astype(BF16), xpair, preferred_element_type=F32))
            y_intra = jnp.where(lane < SSM_HEADDIM, outs[0], outs[1])
            y_ref[:, base:base + LANES] = (y_intra + y_inter[:, pr * LANES:(pr + 1) * LANES]
                                           + xs[:, base:base + LANES] * dex_ref[:, base:base + LANES])
        st_ref[g] = st * declast[:, g * gw:(g + 1) * gw] + jnp.dot(
            bg32.T.astype(BF16), xdec_b[:, g * gw:(g + 1) * gw], preferred_element_type=F32)

    y = y_ref[...] * _silu(p_ref[0, :, 0:di])
    for g in range(SSM_GROUPS):
        sl = slice(g * gw, (g + 1) * gw)
        y_ref[:, sl] = _rms(y[:, sl], gn_ref[:, sl])
    o_ref[0] = h_ref[0] + jnp.dot(y_ref[...].astype(BF16), wout_ref[...], preferred_element_type=F32)


def _ssd_layer(h, g_mix, w_in, conv_w, conv_b, dt_bias, a_log, d_skip, g_norm, w_out):
    b, t, d = h.shape
    n = b * t
    c = min(SSM_CHUNK, t)
    di = SSM_DI
    cdim = di + 2 * SSM_GROUPS * SSM_STATE
    pad_h = LANES - SSM_HEADS
    w_all = jnp.pad(w_in, ((0, 0), (0, pad_h))).astype(BF16)
    fdim = w_all.shape[1]
    proj = _rms_proj(h.reshape(n, d), g_mix, w_all, F32, min(PROJ_TOKENS, n), 896).reshape(b, t, fdim)
    eexp = np.zeros((LANES, di), np.float32)
    for hd in range(SSM_HEADS):
        eexp[hd, hd * SSM_HEADDIM:(hd + 1) * SSM_HEADDIM] = 1.0
    ltri = np.tril(np.ones((c, c), np.float32))
    full = lambda shape: pl.BlockSpec(shape, lambda i, j: tuple(0 for _ in shape))
    return pl.pallas_call(
        functools.partial(_ssd_kernel, c=c),
        grid=(b, t // c),
        in_specs=[pl.BlockSpec((1, c, fdim), lambda i, j: (i, j, 0)),
                  pl.BlockSpec((1, c, d), lambda i, j: (i, j, 0)),
                  full((SSM_CONV, cdim)), full((1, cdim)), full((1, LANES)), full((1, LANES)),
                  full((1, di)), full((1, di)), full((di, d)), full((c, c)), full((LANES, di))],
        out_specs=pl.BlockSpec((1, c, d), lambda i, j: (i, j, 0)),
        out_shape=jax.ShapeDtypeStruct((b, t, d), F32),
        scratch_shapes=[pltpu.VMEM((c + 8, cdim), F32),
                        pltpu.VMEM((SSM_GROUPS, SSM_STATE, di // SSM_GROUPS), F32),
                        pltpu.VMEM((c, di), F32)],
        compiler_params=_cparams("parallel", "arbitrary"),
        name="ssd_scan",
    )(proj, h, conv_w, conv_b.reshape(1, cdim),
      jnp.pad(dt_bias, (0, pad_h)).reshape(1, LANES), jnp.pad(a_log, (0, pad_h)).reshape(1, LANES),
      jnp.repeat(d_skip, SSM_HEADDIM).reshape(1, di), g_norm.reshape(1, di), w_out.astype(BF16),
      jnp.asarray(ltri, BF16), jnp.asarray(eexp, BF16))


def _stack_rows(rows):
    iota16 = lax.broadcasted_iota(jnp.int32, (PEER_TOPK, LANES), 0)
    out = jnp.zeros((PEER_TOPK, LANES), F32)
    for it, r in enumerate(rows):
        out = jnp.where(iota16 == it, r, out)
    return out


def _top16(s, exact):
    nrows = s.shape[0]
    iota = lax.broadcasted_iota(jnp.int32, (nrows, LANES), 0).astype(F32)
    rank = jnp.full((nrows, LANES), float(PEER_TOPK), F32)
    vals = []
    for it in range(PEER_TOPK):
        m = jnp.max(s, axis=0, keepdims=True)
        sel = s == m
        if exact:
            sel = iota == jnp.min(jnp.where(sel, iota, float(nrows)), axis=0, keepdims=True)
        rank = jnp.where(sel, float(it), rank)
        s = jnp.where(sel, -jnp.inf, s)
        vals.append(m)
    return _stack_rows(vals), rank


def _top16_all(scores):
    fast = [_top16(s, False) for s in scores]
    picked = jnp.zeros((1, LANES), F32)
    for _, r in fast:
        picked = picked + jnp.sum(jnp.where(r < float(PEER_TOPK), 1.0, 0.0), axis=0, keepdims=True)
    tied = jnp.max(jnp.abs(picked - float(PEER_TOPK * len(scores)))) > 0.0
    flat = lax.cond(tied,
                    lambda: tuple(x for s in scores for x in _top16(s, True)),
                    lambda: tuple(x for vr in fast for x in vr))
    return [(flat[2 * i], flat[2 * i + 1]) for i in range(len(scores))]


def _row_pair_words(x, psel):
    y = lax.bitcast_convert_type(jnp.dot(psel, x, preferred_element_type=F32), jnp.uint32)
    half = x.shape[0] // 2
    return (y[:half] >> 16) | y[half:]


def _peer_route_kernel(h_ref, g_ref, wq_ref, keys_ref, psel_ref, xn_ref, r2_ref, e2_ref, na_ref, ca_ref,
                       xs_ref, q_ref, *, mt):
    @pl.when(pl.program_id(1) == 0)
    def _():
        xn = _rms(h_ref[...], g_ref[...]).astype(BF16)
        xs_ref[...] = xn
        xn_ref[...] = xn

    q_ref[...] = jnp.dot(xs_ref[...], wq_ref[...], preferred_element_type=F32).astype(BF16)
    k1 = keys_ref[0, 0]
    k2 = keys_ref[0, 1]

    nsub = mt // LANES
    s1s, s2s = [], []
    for sb in range(nsub):
        qs = q_ref[sb * LANES:(sb + 1) * LANES, :]
        s1s.append(lax.dot_general(k1, qs[:, :PEER_HALF], _NT, preferred_element_type=F32))
        s2s.append(lax.dot_general(k2, qs[:, PEER_HALF:], _NT, preferred_element_type=F32))
    tops = _top16_all(s1s + s2s)
    iota16 = lax.broadcasted_iota(jnp.int32, (PEER_TOPK, LANES), 0).astype(F32)

    cnts = [jnp.zeros((PEER_TOPK, LANES), F32) for _ in range(nsub)]
    fronts = [tops[sb][0] + tops[nsub + sb][0][0:1] for sb in range(nsub)]
    for _ in range(PEER_TOPK):
        for sb in range(nsub):
            sv1, sv2, front, cnt = tops[sb][0], tops[nsub + sb][0], fronts[sb], cnts[sb]
            m = jnp.max(front, axis=0, keepdims=True)
            idx = jnp.min(jnp.where(front == m, iota16, float(PEER_TOPK)), axis=0, keepdims=True)
            sel = iota16 == idx
            cnt = jnp.where(sel, cnt + 1.0, cnt)
            n_sel = jnp.max(jnp.where(sel, cnt, 0.0), axis=0, keepdims=True)
            nxt = jnp.max(jnp.where(iota16 == n_sel, sv2, -jnp.inf), axis=0, keepdims=True)
            fronts[sb] = jnp.where(sel, sv1 + nxt, front)
            cnts[sb] = cnt

    for sb in range(nsub):
        sv1, rank1 = tops[sb]
        sv2, rank2 = tops[nsub + sb]
        cnt = cnts[sb]
        s2 = s2s[sb]
        v2 = [sv2[j:j + 1] for j in range(PEER_TOPK)]
        e2r = jnp.exp(sv2 - v2[0])
        pref = jnp.zeros((1, LANES), F32)
        pn = jnp.zeros((PEER_TOPK, LANES), F32)
        for j in range(PEER_TOPK):
            pref = pref + e2r[j:j + 1]
            pn = jnp.where(cnt == float(j + 1), pref, pn)
        e1 = jnp.exp(sv1 - sv1[0:1])
        z = jnp.sum(e1 * pn, axis=0, keepdims=True)
        cw = e1 * (1.0 / z)
        packed = _bf16_bits_hi(cw) | (_bf16_bits_hi(cnt) >> 16)
        sel_w = jnp.zeros((PEER_NKEYS, LANES), jnp.uint32)
        for i in range(PEER_TOPK):
            sel_w = jnp.where(rank1 == float(i), packed[i:i + 1], sel_w)
        r2_ref[sb, 0] = _row_pair_words(rank2.astype(BF16), psel_ref[...])
        e2_ref[sb, 0] = _row_pair_words(jnp.exp(s2 - v2[0]).astype(BF16), psel_ref[...])
        hi = sel_w >> 16
        na_ref[sb, 0] = (sel_w << 16) | (sel_w & 0xFFFF)
        ca_ref[sb, 0] = (hi << 16) | hi


def _bf16_bits_hi(x):
    return lax.bitcast_convert_type(x.astype(BF16).astype(F32), jnp.uint32)


def _packed_rows(words):
    one = pltpu.bitcast(jnp.broadcast_to(words, (BF16_SUBLANES // 2, LANES)), BF16)
    return jnp.concatenate([one] * (LANES // BF16_SUBLANES), axis=0)


def _gelu(x):
    return 0.5 * x * (1.0 + lax.erf(x * (0.5 ** 0.5)))


def _peer_dense_kernel(xn_ref, u_ref, vt_ref, r2_ref, e2_ref, na_ref, ca_ref, h_ref, o_ref,
                       acc_ref, hida_ref, hidb_ref, xua_ref, xub_ref, *, mt, eb, n_eblk):
    e = pl.program_id(0)
    n_pairs = pl.num_programs(0) - 2
    e_gate = lax.rem(jnp.clip(e - 1, 0, n_pairs - 1), n_eblk)
    e_v = lax.rem(jnp.clip(e - 2, 0, n_pairs - 1), n_eblk)

    @pl.when(e == 0)
    def _():
        hida_ref[...] = jnp.zeros_like(hida_ref)
        hidb_ref[...] = jnp.zeros_like(hidb_ref)
        xub_ref[...] = jnp.zeros_like(xub_ref)

    @pl.when(e_v == 0)
    def _():
        acc_ref[...] = jnp.zeros_like(acc_ref)

    zero = jnp.zeros((LANES, LANES), BF16)
    nchunk = eb // PEER_CHUNK
    dch = acc_ref.shape[0] // nchunk

    def step(xu_new, xu_old, hid_new, hid_old):
        def v_matmul(j):
            acc_ref[j * dch:(j + 1) * dch, :] += jnp.dot(vt_ref[0, j * dch:(j + 1) * dch, :], hid_old[...],
                                                         preferred_element_type=F32)

        def u_matmul(j):
            rows = slice(j * PEER_CHUNK, (j + 1) * PEER_CHUNK)
            xu_new[rows, :] = lax.dot_general(u_ref[rows, :], xn_ref[...], _NT,
                                              preferred_element_type=F32)

        def gate_tile(ablk, sb):
            gs = zero
            for hd in range(PEER_HEADS):
                r2 = pltpu.bitcast(r2_ref[sb, hd], BF16)
                e2 = pltpu.bitcast(e2_ref[sb, hd], BF16)
                n_row = _packed_rows(na_ref[sb, hd, 0, ablk:ablk + 1, :])
                c_row = _packed_rows(ca_ref[sb, hd, 0, ablk:ablk + 1, :])
                gs = gs + jnp.where(r2 < n_row, e2, zero) * c_row
            r0 = ablk * LANES
            act = _gelu(xu_old[r0:r0 + LANES, sb * LANES:(sb + 1) * LANES])
            hid_new[r0:r0 + LANES, sb * LANES:(sb + 1) * LANES] = act.astype(BF16) * gs

        nsub = mt // LANES
        mxu_parts = [f for j in range(nchunk) for f in (functools.partial(v_matmul, j),
                                                        functools.partial(u_matmul, j))]
        tiles = [(ablk, sb) for ablk in range(eb // LANES) for sb in range(nsub)]
        per_part = len(tiles) // len(mxu_parts)
        for i, part in enumerate(mxu_parts):
            part()
            for ablk, sb in tiles[i * per_part:(i + 1) * per_part]:
                gate_tile(ablk, sb)

    @pl.when(e % 2 == 0)
    def _():
        step(xua_ref, xub_ref, hidb_ref, hida_ref)

    @pl.when(e % 2 == 1)
    def _():
        step(xub_ref, xua_ref, hida_ref, hidb_ref)

    @pl.when(jnp.logical_and(e_v == n_eblk - 1, e >= 2))
    def _():
        o_ref[...] = h_ref[...] + acc_ref[...].T


def _peer_layer(h2, g_ffn, w_q, keys, u_tab, v_tab, mt, mt_dense, eb):
    n, d = h2.shape
    n_eblk = u_tab.shape[0] // eb
    nsub = mt // LANES
    half = PEER_NKEYS // 2
    rshape_w = jax.ShapeDtypeStruct((n // LANES, PEER_HEADS, PEER_NKEYS, LANES), jnp.uint32)
    rshape_p = jax.ShapeDtypeStruct((n // LANES, PEER_HEADS, half, LANES), jnp.uint32)
    rspec = pl.BlockSpec((nsub, 1, PEER_NKEYS, LANES), lambda i, hd: (i, hd, 0, 0))
    pspec = pl.BlockSpec((nsub, 1, half, LANES), lambda i, hd: (i, hd, 0, 0))
    psel = np.zeros((PEER_NKEYS, PEER_NKEYS), np.float32)
    psel[np.arange(half), 2 * np.arange(half)] = 1.0
    psel[half + np.arange(half), 2 * np.arange(half) + 1] = 1.0
    xn, r2, e2, na, ca = pl.pallas_call(
        functools.partial(_peer_route_kernel, mt=mt),
        grid=(n // mt, PEER_HEADS),
        in_specs=[pl.BlockSpec((mt, d), lambda i, hd: (i, 0)),
                  pl.BlockSpec((1, d), lambda i, hd: (0, 0)),
                  pl.BlockSpec((d, 2 * PEER_HALF), lambda i, hd: (0, hd)),
                  pl.BlockSpec((1, 2, PEER_NKEYS, PEER_HALF), lambda i, hd: (hd, 0, 0, 0)),
                  pl.BlockSpec((PEER_NKEYS, PEER_NKEYS), lambda i, hd: (0, 0))],
        out_specs=[pl.BlockSpec((mt, d), lambda i, hd: (i, 0)), pspec, pspec, rspec, rspec],
        out_shape=[jax.ShapeDtypeStruct((n, d), BF16), rshape_p, rshape_p, rshape_w, rshape_w],
        scratch_shapes=[pltpu.VMEM((mt, d), BF16), pltpu.VMEM((mt, 2 * PEER_HALF), BF16)],
        compiler_params=_cparams("parallel", "arbitrary"),
        name="peer_route",
    )(h2, g_ffn.reshape(1, d), w_q.astype(BF16), keys.astype(BF16), jnp.asarray(psel, BF16))

    mtd = mt_dense
    nsub_d = mtd // LANES
    n_pairs = (n // mtd) * n_eblk
    pair = lambda s, lag: jnp.clip(s - lag, 0, n_pairs - 1)
    n_ablk = eb // LANES
    blocked = lambda x: x.reshape(n // LANES, PEER_HEADS, n_eblk, n_ablk, LANES)
    dspec = pl.BlockSpec((nsub_d, PEER_HEADS, 1, n_ablk, LANES),
                         lambda s: (pair(s, 1) // n_eblk, 0, pair(s, 1) % n_eblk, 0, 0))
    dspec_p = pl.BlockSpec((nsub_d, PEER_HEADS, half, LANES), lambda s: (pair(s, 1) // n_eblk, 0, 0, 0))
    v_blocks = v_tab.astype(BF16).reshape(n_eblk, eb, d).transpose(0, 2, 1)
    return pl.pallas_call(
        functools.partial(_peer_dense_kernel, mt=mtd, eb=eb, n_eblk=n_eblk),
        grid=(n_pairs + 2,),
        in_specs=[pl.BlockSpec((mtd, d), lambda s: (pair(s, 0) // n_eblk, 0)),
                  pl.BlockSpec((eb, d), lambda s: (pair(s, 0) % n_eblk, 0)),
                  pl.BlockSpec((1, d, eb), lambda s: (pair(s, 2) % n_eblk, 0, 0)),
                  dspec_p, dspec_p, dspec, dspec,
                  pl.BlockSpec((mtd, d), lambda s: (pair(s, 2) // n_eblk, 0))],
        out_specs=pl.BlockSpec((mtd, d), lambda s: (pair(s, 2) // n_eblk, 0)),
        out_shape=jax.ShapeDtypeStruct((n, d), F32),
        scratch_shapes=[pltpu.VMEM((d, mtd), F32), pltpu.VMEM((eb, mtd), BF16),
                        pltpu.VMEM((eb, mtd), BF16), pltpu.VMEM((eb, mtd), F32), pltpu.VMEM((eb, mtd), F32)],
        compiler_params=_cparams("arbitrary"),
        name="peer_dense",
    )(xn, u_tab.astype(BF16), v_blocks, r2, e2, blocked(na), blocked(ca), h2)


def kernel(x, norm_mix, norm_ffn, norm_final, hgrn_lb_logits, hgrn_w_in, hgrn_gnorm, hgrn_w_out,
           fox_w_in, fox_b_f, fox_w_out, ssm_w_in, ssm_conv_w, ssm_conv_b, ssm_dt_bias, ssm_a_log,
           ssm_d, ssm_gnorm, ssm_w_out, peer_w_q, peer_keys, peer_u, peer_v):
    b, t, d = x.shape
    n = b * t
    depth = norm_mix.shape[0]
    h = x
    for i in range(depth):
        kind, j = i % 3, i // 3
        if kind == 0:
            proj = _rms_proj(h.reshape(n, d), norm_mix[i], hgrn_w_in[j].astype(BF16), F32,
                             min(PROJ_TOKENS, n), 1024)
            h = _hgrn_layer(h, proj.reshape(b, t, 4 * d), hgrn_lb_logits, hgrn_gnorm[j],
                            hgrn_w_out[j].astype(BF16), i)
        elif kind == 1:
            h = _fox_layer(h, norm_mix[i], fox_w_in[j], fox_b_f[j], fox_w_out[j])
        else:
            h = _ssd_layer(h, norm_mix[i], ssm_w_in[j], ssm_conv_w[j], ssm_conv_b[j], ssm_dt_bias[j],
                           ssm_a_log[j], ssm_d[j], ssm_gnorm[j], ssm_w_out[j])
        h = _peer_layer(h.reshape(n, d), norm_ffn[i], peer_w_q[i], peer_keys[i], peer_u[i],
                        peer_v[i], min(PEER_ROUTE_TOKENS, n), min(PEER_DENSE_TOKENS, n), PEER_DENSE_EXPERTS).reshape(b, t, d)
    return _final_norm(h.reshape(n, d), norm_final, 512).reshape(b, t, d)
```

```python
import functools
import math

import numpy as np
import jax
import jax.numpy as jnp
from jax import lax
from jax.experimental import pallas as pl
from jax.experimental.pallas import tpu as pltpu

F32 = jnp.float32
BF16 = jnp.bfloat16

NORM_EPS = 1e-6
LANES = 128
BF16_SUBLANES = 16
VMEM_LIMIT = 56 * 1024 * 1024
PROJ_TOKENS = 1024

D_MODEL = 1024
HG_HEADS = 8
HG_DK = 128
GLA_CHUNK = 128
GLA_SUB = 16
FOX_HEADS = 16
FOX_DH = 64
FOX_TQ = 1024
FOX_TK = 1024
SSM_DI = 2048
SSM_HEADS = 32
SSM_HEADDIM = 64
SSM_GROUPS = 8
SSM_STATE = 128
SSM_CONV = 4
SSM_CHUNK = 128
PEER_HEADS = 8
PEER_NKEYS = 128
PEER_TOPK = 16
PEER_HALF = 128
PEER_ROUTE_TOKENS = 512
PEER_DENSE_TOKENS = 1024
PEER_DENSE_EXPERTS = 512
PEER_CHUNK = 256

_NT = (((1,), (1,)), ((), ()))
_TN = (((0,), (0,)), ((), ()))


def _cparams(*sem):
    return pltpu.CompilerParams(dimension_semantics=sem, vmem_limit_bytes=VMEM_LIMIT)


def _split3(x):
    hi = x.astype(BF16)
    r = x - hi.astype(F32)
    mid = r.astype(BF16)
    lo = (r - mid.astype(F32)).astype(BF16)
    return hi, mid, lo


def _dot01(m01, x):
    hi, mid, lo = _split3(x)
    d = lambda b: jnp.dot(m01, b, preferred_element_type=F32)
    return d(hi) + d(mid) + d(lo)


def _dot01_r(x, m01):
    hi, mid, lo = _split3(x)
    d = lambda a: jnp.dot(a, m01, preferred_element_type=F32)
    return d(hi) + d(mid) + d(lo)


def _sigmoid(x):
    return 1.0 / (1.0 + jnp.exp(-x))


def _silu(x):
    return x * _sigmoid(x)


def _rms(x, g):
    ms = jnp.mean(x * x, axis=-1, keepdims=True)
    return x * lax.rsqrt(ms + NORM_EPS) * g


def _rms_proj_kernel(x_ref, g_ref, w_ref, o_ref, xn_ref):
    @pl.when(pl.program_id(1) == 0)
    def _():
        xn_ref[...] = _rms(x_ref[...], g_ref[...]).astype(BF16)

    o_ref[...] = jnp.dot(xn_ref[...], w_ref[...], preferred_element_type=F32).astype(o_ref.dtype)


def _rms_proj(x2d, g, w_bf16, out_dtype, tm, tn):
    n, d = x2d.shape
    f = w_bf16.shape[1]
    assert n % tm == 0 and f % tn == 0
    return pl.pallas_call(
        _rms_proj_kernel,
        grid=(n // tm, f // tn),
        in_specs=[pl.BlockSpec((tm, d), lambda i, j: (i, 0)),
                  pl.BlockSpec((1, d), lambda i, j: (0, 0)),
                  pl.BlockSpec((d, tn), lambda i, j: (0, j))],
        out_specs=pl.BlockSpec((tm, tn), lambda i, j: (i, j)),
        out_shape=jax.ShapeDtypeStruct((n, f), out_dtype),
        scratch_shapes=[pltpu.VMEM((tm, d), BF16)],
        compiler_params=_cparams("parallel", "arbitrary"),
        name="rms_proj",
    )(x2d, g.reshape(1, d), w_bf16)


def _final_norm_kernel(x_ref, g_ref, o_ref):
    o_ref[...] = _rms(x_ref[...], g_ref[...])


def _final_norm(x2d, g, tm):
    n, d = x2d.shape
    return pl.pallas_call(
        _final_norm_kernel,
        grid=(n // tm,),
        in_specs=[pl.BlockSpec((tm, d), lambda i: (i, 0)),
                  pl.BlockSpec((1, d), lambda i: (0, 0))],
        out_specs=pl.BlockSpec((tm, d), lambda i: (i, 0)),
        out_shape=jax.ShapeDtypeStruct((n, d), F32),
        compiler_params=_cparams("parallel"),
        name="final_norm",
    )(x2d, g.reshape(1, d))


def _gla_tables(c, sub):
    t = np.arange(c)[:, None]
    j = np.arange(c)[None, :]
    mats = [(j <= t).astype(np.float32)]
    masks = []
    bk = c // 2
    while bk >= sub:
        blk = t // bk
        odd = (blk % 2) == 1
        mats.append(((odd & (j >= blk * bk) & (j <= t))
                     | ((~odd) & (j > t) & (j < (blk + 1) * bk))).astype(np.float32))
        masks.append((odd & ((j // bk) == blk - 1)).astype(np.float32))
        bk //= 2
    m0 = (t // sub) * sub
    mats.append(((j >= m0) & (j <= t)).astype(np.float32))
    masks.append((((t // sub) == (j // sub)) & (j <= t)).astype(np.float32))
    return np.concatenate(mats, axis=0), np.stack(masks, axis=0)


def _hgrn_kernel(proj_ref, h_ref, lbl_ref, gn_ref, wout_ref, dst_ref, msk_ref, o_ref,
                 st_ref, y_ref, *, layer, depth, c, nlev):
    @pl.when(pl.program_id(1) == 0)
    def _():
        st_ref[...] = jnp.zeros_like(st_ref)

    rows = [lbl_ref[j:j + 1, :] for j in range(depth)]
    mx = functools.reduce(jnp.maximum, rows)
    es = [jnp.exp(r - mx) for r in rows]
    tot = functools.reduce(lambda a, b: a + b, es)
    lb = jnp.zeros_like(mx)
    for j in range(1, layer + 1):
        lb = lb + es[j] / tot

    d = HG_HEADS * HG_DK
    q = proj_ref[0, :, 0:d] * (HG_DK ** -0.5)
    f = lb + (1.0 - lb) * _sigmoid(proj_ref[0, :, d:2 * d])
    k = 1.0 - f
    expo = _dot01(dst_ref[...], jnp.log(f))
    b_cum = expo[0:c]
    q_dec = (q * jnp.exp(b_cum)).astype(BF16)
    k_dec = (k * jnp.exp(b_cum[c - 1:c] - b_cum)).astype(BF16)
    dec_last = jnp.exp(b_cum[c - 1:c])
    qs, ks = [], []
    for l in range(nlev - 1):
        w = jnp.exp(expo[(1 + l) * c:(2 + l) * c])
        qs.append((q * w).astype(BF16))
        ks.append((k * w).astype(BF16))
    e_diag = expo[nlev * c:(nlev + 1) * c]
    qs.append((q * jnp.exp(e_diag)).astype(BF16))
    ks.append((k * jnp.exp(-e_diag)).astype(BF16))
    v = proj_ref[0, :, 2 * d:3 * d].astype(BF16)
    gate = proj_ref[0, :, 3 * d:4 * d]
    gn = gn_ref[...]
    for hd in range(HG_HEADS):
        sl = slice(hd * HG_DK, (hd + 1) * HG_DK)
        a = jnp.zeros((c, c), F32)
        for l in range(nlev):
            a = a + lax.dot_general(qs[l][:, sl], ks[l][:, sl], _NT,
                                    preferred_element_type=F32) * msk_ref[l]
        st = st_ref[hd]
        o = (jnp.dot(a.astype(BF16), v[:, sl], preferred_element_type=F32)
             + lax.dot_general(q_dec[:, sl], st.astype(BF16), _NT, preferred_element_type=F32))
        st_ref[hd] = st * dec_last[:, sl] + lax.dot_general(
            v[:, sl], k_dec[:, sl], _TN, preferred_element_type=F32)
        y_ref[:, sl] = (_rms(o, gn) * _silu(gate[:, sl])).astype(BF16)
    o_ref[0] = h_ref[0] + jnp.dot(y_ref[...], wout_ref[...], preferred_element_type=F32)


def _hgrn_layer(h, proj, lb_logits, g_norm, w_out_bf16, layer):
    b, t, d = h.shape
    c = GLA_CHUNK
    dst, msk = _gla_tables(c, GLA_SUB)
    nlev = msk.shape[0]
    depth = lb_logits.shape[0]
    kern = functools.partial(_hgrn_kernel, layer=layer, depth=depth, c=c, nlev=nlev)
    return pl.pallas_call(
        kern,
        grid=(b, t // c),
        in_specs=[pl.BlockSpec((1, c, 4 * d), lambda i, j: (i, j, 0)),
                  pl.BlockSpec((1, c, d), lambda i, j: (i, j, 0)),
                  pl.BlockSpec((depth, d), lambda i, j: (0, 0)),
                  pl.BlockSpec((1, HG_DK), lambda i, j: (0, 0)),
                  pl.BlockSpec((d, d), lambda i, j: (0, 0)),
                  pl.BlockSpec(dst.shape, lambda i, j: (0, 0)),
                  pl.BlockSpec(msk.shape, lambda i, j: (0, 0, 0))],
        out_specs=pl.BlockSpec((1, c, d), lambda i, j: (i, j, 0)),
        out_shape=jax.ShapeDtypeStruct((b, t, d), F32),
        scratch_shapes=[pltpu.VMEM((HG_HEADS, HG_DK, HG_DK), F32),
                        pltpu.VMEM((c, d), BF16)],
        compiler_params=_cparams("parallel", "arbitrary"),
        name="hgrn_gla",
    )(proj, h, lb_logits, g_norm.reshape(1, HG_DK), w_out_bf16,
      jnp.asarray(dst, BF16), jnp.asarray(msk, F32))


def _log_sigmoid(x):
    return jnp.minimum(x, 0.0) - jnp.log1p(jnp.exp(-jnp.abs(x)))


def _fox_cum_kernel(fl_ref, bf_ref, ltri_ref, place_ref, o_ref, carry_ref, *, tc):
    @pl.when(pl.program_id(1) == 0)
    def _():
        carry_ref[...] = jnp.zeros_like(carry_ref)

    lf = _log_sigmoid(fl_ref[0] + bf_ref[...])
    c = _dot01(ltri_ref[...], lf) + carry_ref[...]
    carry_ref[...] = c[tc - 1:tc, :]
    hi, mid, lo = _split3(-c)
    o_ref[0] = jnp.dot(jnp.concatenate([hi, mid, lo], axis=1), place_ref[...],
                       preferred_element_type=F32).astype(BF16)


def _fox_cumsum(f_logit, b_f_pad, tc):
    b, t, _ = f_logit.shape
    ltri = np.tril(np.ones((tc, tc), np.float32))
    place = np.zeros((3 * LANES, FOX_HEADS * LANES), np.float32)
    for hd in range(FOX_HEADS):
        for part in range(3):
            place[part * LANES + hd, hd * LANES + part] = 1.0
    return pl.pallas_call(
        functools.partial(_fox_cum_kernel, tc=tc),
        grid=(b, t // tc),
        in_specs=[pl.BlockSpec((1, tc, LANES), lambda i, j: (i, j, 0)),
                  pl.BlockSpec((1, LANES), lambda i, j: (0, 0)),
                  pl.BlockSpec((tc, tc), lambda i, j: (0, 0)),
                  pl.BlockSpec(place.shape, lambda i, j: (0, 0))],
        out_specs=pl.BlockSpec((1, tc, FOX_HEADS * LANES), lambda i, j: (i, j, 0)),
        out_shape=jax.ShapeDtypeStruct((b, t, FOX_HEADS * LANES), BF16),
        scratch_shapes=[pltpu.VMEM((1, LANES), F32)],
        compiler_params=_cparams("parallel", "arbitrary"),
        name="fox_cumsum",
    )(f_logit, b_f_pad, jnp.asarray(ltri, BF16), jnp.asarray(place, BF16))


def _fox_flash_kernel(q_ref, k_ref, cf_ref, vt_ref, o_ref, qp_ref, m_ref, l_ref, acc_ref, *, tq, tk):
    qi = pl.program_id(2)
    ki = pl.program_id(3)
    nkb = tq // tk
    lane = lax.broadcasted_iota(jnp.int32, (1, LANES), 1)

    @pl.when(ki == 0)
    def _():
        m_ref[...] = jnp.full_like(m_ref, -1e30)
        l_ref[...] = jnp.zeros_like(l_ref)
        acc_ref[...] = jnp.zeros_like(acc_ref)
        q = q_ref[0] * (FOX_DH ** -0.5)
        ones3 = jnp.broadcast_to(jnp.where(lane < 3, 1.0, 0.0).astype(BF16), (tq, LANES))
        for hh in range(2):
            qh = jnp.where((lane < FOX_DH) == (hh == 0), q, jnp.zeros_like(q))
            qp_ref[hh] = jnp.concatenate([qh, ones3], axis=1)

    def body(masked):
        k = k_ref[0]
        st = []
        for hh in range(2):
            kp = jnp.concatenate([k, cf_ref[0, :, hh * LANES:(hh + 1) * LANES]], axis=1)
            st.append(lax.dot_general(kp, qp_ref[hh], _NT, preferred_element_type=F32))
        if masked:
            key_pos = lax.broadcasted_iota(jnp.int32, (tk, tq), 0)
            qry_pos = lax.broadcasted_iota(jnp.int32, (tk, tq), 1)
            if nkb > 1:
                key_pos = key_pos + (ki - qi * nkb) * tk
            causal = key_pos <= qry_pos
        for hh in range(2):
            s = jnp.where(causal, st[hh], -1e30) if masked else st[hh]
            m_prev = m_ref[hh]
            m_new = jnp.maximum(m_prev, jnp.max(s, axis=0, keepdims=True))
            alpha = jnp.exp(m_prev - m_new)
            p = jnp.exp(s - m_new)
            l_ref[hh] = alpha * l_ref[hh] + jnp.sum(p, axis=0, keepdims=True)
            acc_ref[hh] = alpha * acc_ref[hh] + jnp.dot(
                vt_ref[0, hh * FOX_DH:(hh + 1) * FOX_DH, :], p.astype(BF16), preferred_element_type=F32)
            m_ref[hh] = m_new

    @pl.when(ki < qi * nkb)
    def _():
        body(False)

    @pl.when(jnp.logical_and(ki >= qi * nkb, ki < (qi + 1) * nkb))
    def _():
        body(True)

    @pl.when(ki == (qi + 1) * nkb - 1)
    def _():
        o_ref[0] = jnp.concatenate([acc_ref[0] / l_ref[0], acc_ref[1] / l_ref[1]], axis=0).astype(o_ref.dtype)


def _fox_flash(qk, cfeat, v_t, tq, tk):
    b, t, _ = qk.shape
    npair = FOX_HEADS // 2
    nkb = tq // tk
    kv = lambda qi, ki: jnp.minimum(ki, (qi + 1) * nkb - 1)
    return pl.pallas_call(
        functools.partial(_fox_flash_kernel, tq=tq, tk=tk),
        grid=(b, npair, t // tq, t // tk),
        in_specs=[pl.BlockSpec((1, tq, LANES), lambda i, hp, qi, ki: (i, qi, hp)),
                  pl.BlockSpec((1, tk, LANES), lambda i, hp, qi, ki: (i, kv(qi, ki), npair + hp)),
                  pl.BlockSpec((1, tk, 2 * LANES), lambda i, hp, qi, ki: (i, kv(qi, ki), hp)),
                  pl.BlockSpec((1, LANES, tk), lambda i, hp, qi, ki: (i, hp, kv(qi, ki)))],
        out_specs=pl.BlockSpec((1, LANES, tq), lambda i, hp, qi, ki: (i, hp, qi)),
        out_shape=jax.ShapeDtypeStruct((b, D_MODEL, t), BF16),
        scratch_shapes=[pltpu.VMEM((2, tq, 2 * LANES), BF16),
                        pltpu.VMEM((2, 1, tq), F32),
                        pltpu.VMEM((2, 1, tq), F32),
                        pltpu.VMEM((2, FOX_DH, tq), F32)],
        compiler_params=_cparams("parallel", "parallel", "parallel", "arbitrary"),
        name="fox_flash",
    )(qk, qk, cfeat, v_t)


def _rms_proj_t_kernel(x_ref, g_ref, wt_ref, o_ref):
    xn = _rms(x_ref[0], g_ref[...]).astype(BF16)
    o_ref[0] = lax.dot_general(wt_ref[...], xn, _NT, preferred_element_type=F32).astype(o_ref.dtype)


def _rms_proj_t(x, g, wt_bf16, tm):
    b, t, d = x.shape
    f = wt_bf16.shape[0]
    return pl.pallas_call(
        _rms_proj_t_kernel,
        grid=(b, t // tm),
        in_specs=[pl.BlockSpec((1, tm, d), lambda i, j: (i, j, 0)),
                  pl.BlockSpec((1, d), lambda i, j: (0, 0)),
                  pl.BlockSpec((f, d), lambda i, j: (0, 0))],
        out_specs=pl.BlockSpec((1, f, tm), lambda i, j: (i, 0, j)),
        out_shape=jax.ShapeDtypeStruct((b, f, t), BF16),
        compiler_params=_cparams("parallel", "parallel"),
        name="rms_proj_t",
    )(x, g.reshape(1, d), wt_bf16)


def _mm_t_res_kernel(xt_ref, w_ref, r_ref, o_ref):
    o_ref[0] = r_ref[0] + lax.dot_general(xt_ref[0], w_ref[...], _TN, preferred_element_type=F32)


def _mm_t_res(x_t, w_bf16, res, tm):
    b, k, t = x_t.shape
    d = w_bf16.shape[1]
    return pl.pallas_call(
        _mm_t_res_kernel,
        grid=(b, t // tm),
        in_specs=[pl.BlockSpec((1, k, tm), lambda i, j: (i, 0, j)),
                  pl.BlockSpec((k, d), lambda i, j: (0, 0)),
                  pl.BlockSpec((1, tm, d), lambda i, j: (i, j, 0))],
        out_specs=pl.BlockSpec((1, tm, d), lambda i, j: (i, j, 0)),
        out_shape=jax.ShapeDtypeStruct((b, t, d), F32),
        compiler_params=_cparams("parallel", "parallel"),
        name="mm_t_res",
    )(x_t, w_bf16, res)


def _fox_layer(h, g_mix, w_in, b_f, w_out):
    b, t, d = h.shape
    n = b * t
    h2 = h.reshape(n, d)
    w_qk = w_in[:, :2 * d].astype(BF16)
    w_vt = w_in[:, 2 * d:3 * d].T.astype(BF16)
    w_f = jnp.pad(w_in[:, 3 * d:], ((0, 0), (0, LANES - FOX_HEADS))).astype(BF16)
    qk = _rms_proj(h2, g_mix, w_qk, BF16, min(PROJ_TOKENS, n), 1024).reshape(b, t, 2 * d)
    v_t = _rms_proj_t(h, g_mix, w_vt, min(512, t))
    fl = _rms_proj(h2, g_mix, w_f, F32, 512, LANES).reshape(b, t, LANES)
    b_f_pad = jnp.pad(b_f, (0, LANES - FOX_HEADS)).reshape(1, LANES)
    cfeat = _fox_cumsum(fl, b_f_pad, min(512, t))
    o_t = _fox_flash(qk, cfeat, v_t, min(FOX_TQ, t), min(FOX_TK, t))
    return _mm_t_res(o_t, w_out.astype(BF16), h, min(512, t))


def _softplus(x):
    return jnp.maximum(x, 0.0) + jnp.log1p(jnp.exp(-jnp.abs(x)))


def _ssd_kernel(p_ref, h_ref, cw_ref, cb_ref, dtb_ref, alog_ref, dex_ref, gn_ref, wout_ref,
                ltri_ref, eexp_ref, o_ref, buf_ref, st_ref, y_ref, *, c):
    di = SSM_DI
    gn_w = SSM_GROUPS * SSM_STATE
    gw = di // SSM_GROUPS

    @pl.when(pl.program_id(1) == 0)
    def _():
        buf_ref[0:8, :] = jnp.zeros((8, buf_ref.shape[1]), F32)
        st_ref[...] = jnp.zeros_like(st_ref)

    buf_ref[8:8 + c, :] = p_ref[0, :, di:di + di + 2 * gn_w]
    conv = cb_ref[...]
    for kk in range(SSM_CONV):
        conv = conv + cw_ref[kk:kk + 1, :] * buf_ref[5 + kk:5 + kk + c, :]
    buf_ref[0:8, :] = buf_ref[c:c + 8, :]
    xbc = _silu(conv)
    xs = xbc[:, :di]
    bm = xbc[:, di:di + gn_w]
    cm = xbc[:, di + gn_w:]

    dt = _softplus(p_ref[0, :, 2 * di + 2 * gn_w:] + dtb_ref[...])
    da = dt * (-jnp.exp(alog_ref[...]))
    cum = _dot01(ltri_ref[...], da)
    cum_t = cum.T
    ex = _dot01_r(jnp.concatenate([cum, dt], axis=0), eexp_ref[...])
    cum_e = ex[:c]
    dt_e = ex[c:]
    last = cum_e[c - 1:c]
    expcum = jnp.exp(cum_e)
    declast = jnp.exp(last)
    xdt = xs * dt_e
    xdt_b = xdt.astype(BF16)
    xdec_b = (xdt * jnp.exp(last - cum_e)).astype(BF16)

    row = lax.broadcasted_iota(jnp.int32, (c, c), 0)
    col = lax.broadcasted_iota(jnp.int32, (c, c), 1)
    tril = row >= col
    lane = lax.broadcasted_iota(jnp.int32, (1, LANES), 1)
    for g in range(SSM_GROUPS):
        cg = cm[:, g * SSM_STATE:(g + 1) * SSM_STATE].astype(BF16)
        bg32 = bm[:, g * SSM_STATE:(g + 1) * SSM_STATE]
        cb = lax.dot_general(cg, bg32.astype(BF16), _NT, preferred_element_type=F32)
        st = st_ref[g]
        y_inter = (jnp.dot(cg, st.astype(BF16), preferred_element_type=F32)
                   * expcum[:, g * gw:(g + 1) * gw])
        for pr in range(2):
            base = g * gw + pr * LANES
            xpair = xdt_b[:, base:base + LANES]
            outs = []
            for hh in range(2):
                hidx = g * 4 + pr * 2 + hh
                diff = cum[:, hidx:hidx + 1] - cum_t[hidx:hidx + 1, :]
                lmat = jnp.exp(jnp.where(tril, diff, -1e30))
                outs.append(jnp.dot((cb * lmat).astype(BF16), xpair, preferred_element_type=F32))
            y_intra = jnp.where(lane < SSM_HEADDIM, outs[0], outs[1])
            y_ref[:, base:base + LANES] = (y_intra + y_inter[:, pr * LANES:(pr + 1) * LANES]
                                           + xs[:, base:base + LANES] * dex_ref[:, base:base + LANES])
        st_ref[g] = st * declast[:, g * gw:(g + 1) * gw] + jnp.dot(
            bg32.T.astype(BF16), xdec_b[:, g * gw:(g + 1) * gw], preferred_element_type=F32)

    y = y_ref[...] * _silu(p_ref[0, :, 0:di])
    for g in range(SSM_GROUPS):
        sl = slice(g * gw, (g + 1) * gw)
        y_ref[:, sl] = _rms(y[:, sl], gn_ref[:, sl])
    o_ref[0] = h_ref[0] + jnp.dot(y_ref[...].astype(BF16), wout_ref[...], preferred_element_type=F32)


def _ssd_layer(h, g_mix, w_in, conv_w, conv_b, dt_bias, a_log, d_skip, g_norm, w_out):
    b, t, d = h.shape
    n = b * t
    c = min(SSM_CHUNK, t)
    di = SSM_DI
    cdim = di + 2 * SSM_GROUPS * SSM_STATE
    pad_h = LANES - SSM_HEADS
    w_all = jnp.pad(w_in, ((0, 0), (0, pad_h))).astype(BF16)
    fdim = w_all.shape[1]
    proj = _rms_proj(h.reshape(n, d), g_mix, w_all, F32, min(PROJ_TOKENS, n), 896).reshape(b, t, fdim)
    eexp = np.zeros((LANES, di), np.float32)
    for hd in range(SSM_HEADS):
        eexp[hd, hd * SSM_HEADDIM:(hd + 1) * SSM_HEADDIM] = 1.0
    ltri = np.tril(np.ones((c, c), np.float32))
    full = lambda shape: pl.BlockSpec(shape, lambda i, j: tuple(0 for _ in shape))
    return pl.pallas_call(
        functools.partial(_ssd_kernel, c=c),
        grid=(b, t // c),
        in_specs=[pl.BlockSpec((1, c, fdim), lambda i, j: (i, j, 0)),
                  pl.BlockSpec((1, c, d), lambda i, j: (i, j, 0)),
                  full((SSM_CONV, cdim)), full((1, cdim)), full((1, LANES)), full((1, LANES)),
                  full((1, di)), full((1, di)), full((di, d)), full((c, c)), full((LANES, di))],
        out_specs=pl.BlockSpec((1, c, d), lambda i, j: (i, j, 0)),
        out_shape=jax.ShapeDtypeStruct((b, t, d), F32),
        scratch_shapes=[pltpu.VMEM((c + 8, cdim), F32),
                        pltpu.VMEM((SSM_GROUPS, SSM_STATE, di // SSM_GROUPS), F32),
                        pltpu.VMEM((c, di), F32)],
        compiler_params=_cparams("parallel", "arbitrary"),
        name="ssd_scan",
    )(proj, h, conv_w, conv_b.reshape(1, cdim),
      jnp.pad(dt_bias, (0, pad_h)).reshape(1, LANES), jnp.pad(a_log, (0, pad_h)).reshape(1, LANES),
      jnp.repeat(d_skip, SSM_HEADDIM).reshape(1, di), g_norm.reshape(1, di), w_out.astype(BF16),
      jnp.asarray(ltri, BF16), jnp.asarray(eexp, BF16))


def _stack_rows(rows):
    iota16 = lax.broadcasted_iota(jnp.int32, (PEER_TOPK, LANES), 0)
    out = jnp.zeros((PEER_TOPK, LANES), F32)
    for it, r in enumerate(rows):
        out = jnp.where(iota16 == it, r, out)
    return out


def _top16(s, exact):
    nrows = s.shape[0]
    iota = lax.broadcasted_iota(jnp.int32, (nrows, LANES), 0).astype(F32)
    rank = jnp.full((nrows, LANES), float(PEER_TOPK), F32)
    vals = []
    for it in range(PEER_TOPK):
        m = jnp.max(s, axis=0, keepdims=True)
        sel = s == m
        if exact:
            sel = iota == jnp.min(jnp.where(sel, iota, float(nrows)), axis=0, keepdims=True)
        rank = jnp.where(sel, float(it), rank)
        s = jnp.where(sel, -jnp.inf, s)
        vals.append(m)
    return _stack_rows(vals), rank


def _top16_all(scores):
    fast = [_top16(s, False) for s in scores]
    picked = jnp.zeros((1, LANES), F32)
    for _, r in fast:
        picked = picked + jnp.sum(jnp.where(r < float(PEER_TOPK), 1.0, 0.0), axis=0, keepdims=True)
    tied = jnp.max(jnp.abs(picked - float(PEER_TOPK * len(scores)))) > 0.0
    flat = lax.cond(tied,
                    lambda: tuple(x for s in scores for x in _top16(s, True)),
                    lambda: tuple(x for vr in fast for x in vr))
    return [(flat[2 * i], flat[2 * i + 1]) for i in range(len(scores))]


def _row_pair_words(x, psel):
    y = lax.bitcast_convert_type(jnp.dot(psel, x, preferred_element_type=F32), jnp.uint32)
    half = x.shape[0] // 2
    return (y[:half] >> 16) | y[half:]


def _peer_route_kernel(h_ref, g_ref, wq_ref, keys_ref, psel_ref, xn_ref, r2_ref, e2_ref, na_ref, ca_ref,
                       xs_ref, q_ref, *, mt):
    @pl.when(pl.program_id(1) == 0)
    def _():
        xn = _rms(h_ref[...], g_ref[...]).astype(BF16)
        xs_ref[...] = xn
        xn_ref[...] = xn

    q_ref[...] = jnp.dot(xs_ref[...], wq_ref[...], preferred_element_type=F32).astype(BF16)
    k1 = keys_ref[0, 0]
    k2 = keys_ref[0, 1]

    nsub = mt // LANES
    s1s, s2s = [], []
    for sb in range(nsub):
        qs = q_ref[sb * LANES:(sb + 1) * LANES, :]
        s1s.append(lax.dot_general(k1, qs[:, :PEER_HALF], _NT, preferred_element_type=F32))
        s2s.append(lax.dot_general(k2, qs[:, PEER_HALF:], _NT, preferred_element_type=F32))
    tops = _top16_all(s1s + s2s)
    iota16 = lax.broadcasted_iota(jnp.int32, (PEER_TOPK, LANES), 0).astype(F32)

    cnts = [jnp.zeros((PEER_TOPK, LANES), F32) for _ in range(nsub)]
    fronts = [tops[sb][0] + tops[nsub + sb][0][0:1] for sb in range(nsub)]
    for _ in range(PEER_TOPK):
        for sb in range(nsub):
            sv1, sv2, front, cnt = tops[sb][0], tops[nsub + sb][0], fronts[sb], cnts[sb]
            m = jnp.max(front, axis=0, keepdims=True)
            idx = jnp.min(jnp.where(front == m, iota16, float(PEER_TOPK)), axis=0, keepdims=True)
            sel = iota16 == idx
            cnt = jnp.where(sel, cnt + 1.0, cnt)
            n_sel = jnp.max(jnp.where(sel, cnt, 0.0), axis=0, keepdims=True)
            nxt = jnp.max(jnp.where(iota16 == n_sel, sv2, -jnp.inf), axis=0, keepdims=True)
            fronts[sb] = jnp.where(sel, sv1 + nxt, front)
            cnts[sb] = cnt

    for sb in range(nsub):
        sv1, rank1 = tops[sb]
        sv2, rank2 = tops[nsub + sb]
        cnt = cnts[sb]
        s2 = s2s[sb]
        v2 = [sv2[j:j + 1] for j in range(PEER_TOPK)]
        e2r = jnp.exp(sv2 - v2[0])
        pref = jnp.zeros((1, LANES), F32)
        pn = jnp.zeros((PEER_TOPK, LANES), F32)
        for j in range(PEER_TOPK):
            pref = pref + e2r[j:j + 1]
            pn = jnp.where(cnt == float(j + 1), pref, pn)
        e1 = jnp.exp(sv1 - sv1[0:1])
        z = jnp.sum(e1 * pn, axis=0, keepdims=True)
        cw = e1 * (1.0 / z)
        packed = _bf16_bits_hi(cw) | (_bf16_bits_hi(cnt) >> 16)
        sel_w = jnp.zeros((PEER_NKEYS, LANES), jnp.uint32)
        for i in range(PEER_TOPK):
            sel_w = jnp.where(rank1 == float(i), packed[i:i + 1], sel_w)
        r2_ref[sb, 0] = _row_pair_words(rank2.astype(BF16), psel_ref[...])
        e2_ref[sb, 0] = _row_pair_words(jnp.exp(s2 - v2[0]).astype(BF16), psel_ref[...])
        hi = sel_w >> 16
        na_ref[sb, 0] = (sel_w << 16) | (sel_w & 0xFFFF)
        ca_ref[sb, 0] = (hi << 16) | hi


def _bf16_bits_hi(x):
    return lax.bitcast_convert_type(x.astype(BF16).astype(F32), jnp.uint32)


def _packed_rows(words):
    one = pltpu.bitcast(jnp.broadcast_to(words, (BF16_SUBLANES // 2, LANES)), BF16)
    return jnp.concatenate([one] * (LANES // BF16_SUBLANES), axis=0)


def _gelu(x):
    return 0.5 * x * (1.0 + lax.erf(x * (0.5 ** 0.5)))


def _peer_dense_kernel(xn_ref, u_ref, vt_ref, r2_ref, e2_ref, na_ref, ca_ref, h_ref, o_ref,
                       acc_ref, hida_ref, hidb_ref, xua_ref, xub_ref, *, mt, eb, n_eblk):
    e = pl.program_id(0)
    n_pairs = pl.num_programs(0) - 2
    e_gate = lax.rem(jnp.clip(e - 1, 0, n_pairs - 1), n_eblk)
    e_v = lax.rem(jnp.clip(e - 2, 0, n_pairs - 1), n_eblk)

    @pl.when(e == 0)
    def _():
        hida_ref[...] = jnp.zeros_like(hida_ref)
        hidb_ref[...] = jnp.zeros_like(hidb_ref)
        xub_ref[...] = jnp.zeros_like(xub_ref)

    @pl.when(e_v == 0)
    def _():
        acc_ref[...] = jnp.zeros_like(acc_ref)

    zero = jnp.zeros((LANES, LANES), BF16)
    nchunk = eb // PEER_CHUNK
    dch = acc_ref.shape[0] // nchunk

    def step(xu_new, xu_old, hid_new, hid_old):
        def v_matmul(j):
            acc_ref[j * dch:(j + 1) * dch, :] += jnp.dot(vt_ref[0, j * dch:(j + 1) * dch, :], hid_old[...],
                                                         preferred_element_type=F32)

        def u_matmul(j):
            rows = slice(j * PEER_CHUNK, (j + 1) * PEER_CHUNK)
            xu_new[rows, :] = lax.dot_general(u_ref[rows, :], xn_ref[...], _NT,
                                              preferred_element_type=F32)

        def gate_tile(ablk, sb):
            gs = zero
            for hd in range(PEER_HEADS):
                r2 = pltpu.bitcast(r2_ref[sb, hd], BF16)
                e2 = pltpu.bitcast(e2_ref[sb, hd], BF16)
                n_row = _packed_rows(na_ref[sb, hd, 0, ablk:ablk + 1, :])
                c_row = _packed_rows(ca_ref[sb, hd, 0, ablk:ablk + 1, :])
                gs = gs + jnp.where(r2 < n_row, e2, zero) * c_row
            r0 = ablk * LANES
            act = _gelu(xu_old[r0:r0 + LANES, sb * LANES:(sb + 1) * LANES])
            hid_new[r0:r0 + LANES, sb * LANES:(sb + 1) * LANES] = act.astype(BF16) * gs

        nsub = mt // LANES
        mxu_parts = [f for j in range(nchunk) for f in (functools.partial(v_matmul, j),
                                                        functools.partial(u_matmul, j))]
        tiles = [(ablk, sb) for ablk in range(eb // LANES) for sb in range(nsub)]
        per_part = len(tiles) // len(mxu_parts)
        for i, part in enumerate(mxu_parts):
            part()
            for ablk, sb in tiles[i * per_part:(i + 1) * per_part]:
                gate_tile(ablk, sb)

    @pl.when(e % 2 == 0)
    def _():
        step(xua_ref, xub_ref, hidb_ref, hida_ref)

    @pl.when(e % 2 == 1)
    def _():
        step(xub_ref, xua_ref, hida_ref, hidb_ref)

    @pl.when(jnp.logical_and(e_v == n_eblk - 1, e >= 2))
    def _():
        o_ref[...] = h_ref[...] + acc_ref[...].T


def _peer_layer(h2, g_ffn, w_q, keys, u_tab, v_tab, mt, mt_dense, eb):
    n, d = h2.shape
    n_eblk = u_tab.shape[0] // eb
    nsub = mt // LANES
    half = PEER_NKEYS // 2
    rshape_w = jax.ShapeDtypeStruct((n // LANES, PEER_HEADS, PEER_NKEYS, LANES), jnp.uint32)
    rshape_p = jax.ShapeDtypeStruct((n // LANES, PEER_HEADS, half, LANES), jnp.uint32)
    rspec = pl.BlockSpec((nsub, 1, PEER_NKEYS, LANES), lambda i, hd: (i, hd, 0, 0))
    pspec = pl.BlockSpec((nsub, 1, half, LANES), lambda i, hd: (i, hd, 0, 0))
    psel = np.zeros((PEER_NKEYS, PEER_NKEYS), np.float32)
    psel[np.arange(half), 2 * np.arange(half)] = 1.0
    psel[half + np.arange(half), 2 * np.arange(half) + 1] = 1.0
    xn, r2, e2, na, ca = pl.pallas_call(
        functools.partial(_peer_route_kernel, mt=mt),
        grid=(n // mt, PEER_HEADS),
        in_specs=[pl.BlockSpec((mt, d), lambda i, hd: (i, 0)),
                  pl.BlockSpec((1, d), lambda i, hd: (0, 0)),
                  pl.BlockSpec((d, 2 * PEER_HALF), lambda i, hd: (0, hd)),
                  pl.BlockSpec((1, 2, PEER_NKEYS, PEER_HALF), lambda i, hd: (hd, 0, 0, 0)),
                  pl.BlockSpec((PEER_NKEYS, PEER_NKEYS), lambda i, hd: (0, 0))],
        out_specs=[pl.BlockSpec((mt, d), lambda i, hd: (i, 0)), pspec, pspec, rspec, rspec],
        out_shape=[jax.ShapeDtypeStruct((n, d), BF16), rshape_p, rshape_p, rshape_w, rshape_w],
        scratch_shapes=[pltpu.VMEM((mt, d), BF16), pltpu.VMEM((mt, 2 * PEER_HALF), BF16)],
        compiler_params=_cparams("parallel", "arbitrary"),
        name="peer_route",
    )(h2, g_ffn.reshape(1, d), w_q.astype(BF16), keys.astype(BF16), jnp.asarray(psel, BF16))

    mtd = mt_dense
    nsub_d = mtd // LANES
    n_pairs = (n // mtd) * n_eblk
    pair = lambda s, lag: jnp.clip(s - lag, 0, n_pairs - 1)
    n_ablk = eb // LANES
    blocked = lambda x: x.reshape(n // LANES, PEER_HEADS, n_eblk, n_ablk, LANES)
    dspec = pl.BlockSpec((nsub_d, PEER_HEADS, 1, n_ablk, LANES),
                         lambda s: (pair(s, 1) // n_eblk, 0, pair(s, 1) % n_eblk, 0, 0))
    dspec_p = pl.BlockSpec((nsub_d, PEER_HEADS, half, LANES), lambda s: (pair(s, 1) // n_eblk, 0, 0, 0))
    v_blocks = v_tab.astype(BF16).reshape(n_eblk, eb, d).transpose(0, 2, 1)
    return pl.pallas_call(
        functools.partial(_peer_dense_kernel, mt=mtd, eb=eb, n_eblk=n_eblk),
        grid=(n_pairs + 2,),
        in_specs=[pl.BlockSpec((mtd, d), lambda s: (pair(s, 0) // n_eblk, 0)),
                  pl.BlockSpec((eb, d), lambda s: (pair(s, 0) % n_eblk, 0)),
                  pl.BlockSpec((1, d, eb), lambda s: (pair(s, 2) % n_eblk, 0, 0)),
                  dspec_p, dspec_p, dspec, dspec,
                  pl.BlockSpec((mtd, d), lambda s: (pair(s, 2) // n_eblk, 0))],
        out_specs=pl.BlockSpec((mtd, d), lambda s: (pair(s, 2) // n_eblk, 0)),
        out_shape=jax.ShapeDtypeStruct((n, d), F32),
        scratch_shapes=[pltpu.VMEM((d, mtd), F32), pltpu.VMEM((eb, mtd), BF16),
                        pltpu.VMEM((eb, mtd), BF16), pltpu.VMEM((eb, mtd), F32), pltpu.VMEM((eb, mtd), F32)],
        compiler_params=_cparams("arbitrary"),
        name="peer_dense",
    )(xn, u_tab.astype(BF16), v_blocks, r2, e2, blocked(na), blocked(ca), h2)


def kernel(x, norm_mix, norm_ffn, norm_final, hgrn_lb_logits, hgrn_w_in, hgrn_gnorm, hgrn_w_out,
           fox_w_in, fox_b_f, fox_w_out, ssm_w_in, ssm_conv_w, ssm_conv_b, ssm_dt_bias, ssm_a_log,
           ssm_d, ssm_gnorm, ssm_w_out, peer_w_q, peer_keys, peer_u, peer_v):
    b, t, d = x.shape
    n = b * t
    depth = norm_mix.shape[0]
    h = x
    for i in range(depth):
        kind, j = i % 3, i // 3
        if kind == 0:
            proj = _rms_proj(h.reshape(n, d), norm_mix[i], hgrn_w_in[j].astype(BF16), F32,
                             min(PROJ_TOKENS, n), 1024)
            h = _hgrn_layer(h, proj.reshape(b, t, 4 * d), hgrn_lb_logits, hgrn_gnorm[j],
                            hgrn_w_out[j].astype(BF16), i)
        elif kind == 1:
            h = _fox_layer(h, norm_mix[i], fox_w_in[j], fox_b_f[j], fox_w_out[j])
        else:
            h = _ssd_layer(h, norm_mix[i], ssm_w_in[j], ssm_conv_w[j], ssm_conv_b[j], ssm_dt_bias[j],
                           ssm_a_log[j], ssm_d[j], ssm_gnorm[j], ssm_w_out[j])
        h = _peer_layer(h.reshape(n, d), norm_ffn[i], peer_w_q[i], peer_keys[i], peer_u[i],
                        peer_v[i], min(PEER_ROUTE_TOKENS, n), min(PEER_DENSE_TOKENS, n), PEER_DENSE_EXPERTS).reshape(b, t, d)
    return _final_norm(h.reshape(n, d), norm_final, 512).reshape(b, t, d)
```
